```python
import math
import jax, jax.numpy as jnp
from jax import lax
import numpy as np

D_MODEL = 1024
BATCH = 8
SEQ = 2048
DEPTH = 4
DEC_BATCH = 32
DEC_SEQ = 4
PAST_LEN = 16384
PAGE_SIZE = 128

HEAD_DIM = 64
ROT_DIM = HEAD_DIM // 4
ROPE_THETA = 500000.0
FOX_HEADS = 8
FOX_KV_HEADS = 2
FORGET_BIAS_INIT = 4.0
MLA_HEADS = 8
MLA_NOPE = 64
MLA_ROPE = 32
MLA_V = 64
MLA_Q_LORA = 384
MLA_KV_LORA = 256
DSA_HEADS = 8
DSA_KV_HEADS = 2
IDX_HEADS = 8
IDX_DIM = 64
IDX_ROT = IDX_DIM // 4
TOPK_MAX = 256
MEM_HEADS = 4
MEM_LEN = 256
N_BRANCH = 4
Q_BLOCK = 128
FOX_W = FOX_HEADS * HEAD_DIM
MLA_W = MLA_HEADS * MLA_V
DSA_W = DSA_HEADS * HEAD_DIM
MEM_W = MEM_HEADS * HEAD_DIM
BRANCH_WIDTHS = (FOX_W, MLA_W, DSA_W, MEM_W)
BRANCH_TOTAL = FOX_W + MLA_W + DSA_W + MEM_W
IN_NAMES = ('fox_q', 'fox_k', 'fox_v', 'fox_f', 'fox_z',
            'mla_qa', 'mla_kva', 'mla_kr', 'mla_z',
            'dsa_q', 'dsa_k', 'dsa_v', 'idx_q', 'idx_k', 'idx_w', 'dsa_z',
            'mem_q', 'mem_z', 'gates')
IN_SPLITS = (FOX_W, FOX_KV_HEADS * HEAD_DIM, FOX_KV_HEADS * HEAD_DIM, FOX_HEADS, FOX_W,
             MLA_Q_LORA, MLA_KV_LORA, MLA_ROPE, MLA_W,
             DSA_W, DSA_KV_HEADS * HEAD_DIM, DSA_KV_HEADS * HEAD_DIM, IDX_HEADS * IDX_DIM, IDX_DIM, IDX_HEADS, DSA_W,
             MEM_W, MEM_W,
             N_BRANCH * D_MODEL)
IN_WIDTH = sum(IN_SPLITS)
ALPHA = (2 * DEPTH) ** 0.25
BETA = (8 * DEPTH) ** -0.25
FOX_SCALE = HEAD_DIM ** -0.5
MLA_SCALE = (MLA_NOPE + MLA_ROPE) ** -0.5
DSA_SCALE = HEAD_DIM ** -0.5
MEM_SCALE = HEAD_DIM ** -0.5
IDX_SCALE = IDX_DIM ** -0.5
IDX_W_SCALE = IDX_HEADS ** -0.5
LN_EPS = 1e-5
RMS_EPS = 1e-6

kernel_name = 'hybrid_fox_mla_dsa_memory_decoder_step'


def layer_norm(x, g, b):
    xf = x.astype(jnp.float32)
    mu = jnp.mean(xf, axis=-1, keepdims=True)
    var = jnp.mean(jnp.square(xf - mu), axis=-1, keepdims=True)
    return ((xf - mu) * lax.rsqrt(var + LN_EPS) * g + b).astype(x.dtype)


def rms_norm(x, g):
    xf = x.astype(jnp.float32)
    return (xf * lax.rsqrt(jnp.mean(xf * xf, axis=-1, keepdims=True) + RMS_EPS) * g).astype(x.dtype)


def rotary(x, pos, rot_dim):
    half = rot_dim // 2
    inv_freq = jnp.exp(-math.log(ROPE_THETA) * jnp.arange(half, dtype=jnp.float32) / half)
    ang = pos.astype(jnp.float32)[:, None] * inv_freq[None, :]
    cos = jnp.cos(ang)[:, None, :]
    sin = jnp.sin(ang)[:, None, :]
    xr = x[..., :rot_dim].astype(jnp.float32)
    x1, x2 = xr[..., :half], xr[..., half:]
    rot = jnp.concatenate([x1 * cos - x2 * sin, x2 * cos + x1 * sin], axis=-1).astype(x.dtype)
    return jnp.concatenate([rot, x[..., rot_dim:]], axis=-1)


def gather_rows(a, idx):
    return jax.vmap(lambda ab, ib: ab[ib])(a, idx)


def project(x, pos, w_in_l, b_f, q_norm, kv_norm, w_q_up, w_kv_up):
    B, T, _ = x.shape
    cuts = np.cumsum(IN_SPLITS)[:-1].tolist()
    h = jnp.einsum('btd,de->bte', x, w_in_l)
    (fq, fk, fv, ff, fz, mqa, mkva, mkr, mz,
     dq, dk, dv, iq, ik, iw, dz, meq, mez, gts) = jnp.split(h, cuts, axis=-1)
    heads = lambda a, n: a.reshape(B, T, n, a.shape[-1] // n)
    q = jnp.einsum('btc,ce->bte', rms_norm(mqa, q_norm), w_q_up).reshape(B, T, MLA_HEADS, MLA_NOPE + MLA_ROPE)
    q_lat = jnp.einsum('bthn,chn->bthc', q[..., :MLA_NOPE], w_kv_up[..., :MLA_NOPE])
    return dict(
        fox_q=heads(fq, FOX_HEADS), fox_k=heads(fk, FOX_KV_HEADS), fox_v=heads(fv, FOX_KV_HEADS),
        fox_logf=jax.nn.log_sigmoid(ff.astype(jnp.float32) + b_f.astype(jnp.float32)),
        fox_z=fz,
        mla_qlat=q_lat, mla_qrope=rotary(q[..., MLA_NOPE:], pos, MLA_ROPE),
        mla_ckv=rms_norm(mkva, kv_norm), mla_krope=rotary(mkr[:, :, None, :], pos, MLA_ROPE)[:, :, 0, :],
        mla_z=mz,
        dsa_q=rotary(heads(dq, DSA_HEADS), pos, ROT_DIM), dsa_k=rotary(heads(dk, DSA_KV_HEADS), pos, ROT_DIM),
        dsa_v=heads(dv, DSA_KV_HEADS),
        idx_q=rotary(heads(iq, IDX_HEADS), pos, IDX_ROT), idx_k=rotary(ik[:, :, None, :], pos, IDX_ROT)[:, :, 0, :],
        idx_w=iw * IDX_W_SCALE, dsa_z=dz,
        mem_q=heads(meq, MEM_HEADS), mem_z=mez,
        gates=gts)


def blocked_attention(qk_pairs, v, q_pos, k_pos, scale, q_logc=None, k_logc=None):
    B, Tq, H, _ = qk_pairs[0][0].shape
    G = v.shape[2]
    R = H // G
    dv = v.shape[-1]
    qb = Q_BLOCK if Tq % Q_BLOCK == 0 else Tq
    nb = Tq // qb
    to_blocks = lambda a: jnp.moveaxis(a.reshape((B, nb, qb, G, R) + a.shape[3:]), 1, 0)
    qs = [to_blocks(q) for q, _ in qk_pairs]
    ks = [k for _, k in qk_pairs]
    use_decay = q_logc is not None
    cq = to_blocks(q_logc) if use_decay else None
    ck = jnp.moveaxis(k_logc.reshape(B, -1, G, R), 1, 3) if use_decay else None
    pb = q_pos.reshape(nb, qb)

    def block(args):
        q_blk, p_blk, c_blk = args
        s = sum(jnp.einsum('bqgrd,bkgd->bgrqk', qq, kk, preferred_element_type=jnp.float32)
                for qq, kk in zip(q_blk, ks)) * scale
        if use_decay:
            s = s + (jnp.moveaxis(c_blk, 1, 3)[..., None] - ck[..., None, :]).astype(jnp.float32)
        s = jnp.where(k_pos[None, :] <= p_blk[:, None], s, -jnp.inf)
        p = jax.nn.softmax(s, axis=-1).astype(v.dtype)
        return jnp.einsum('bgrqk,bkgd->bqgrd', p, v)

    out = lax.map(block, (qs, pb, cq))
    return jnp.moveaxis(out, 0, 1).reshape(B, Tq, H, dv)


def mla_attend(q_lat, q_rope, ckv, krope, q_pos, k_pos, w_kv_up_l):
    o_lat = blocked_attention([(q_lat, ckv[:, :, None, :]), (q_rope, krope[:, :, None, :])],
                              ckv[:, :, None, :], q_pos, k_pos, MLA_SCALE)
    return jnp.einsum('bthc,chv->bthv', o_lat, w_kv_up_l[..., MLA_NOPE:])


def dsa_attention(q, iq, iw, ik, q_pos, k_pos, fetch):
    B, Tq, H, d = q.shape
    G = DSA_KV_HEADS
    R = H // G
    Tk = ik.shape[1]
    topk = min(TOPK_MAX, Tk // 4)
    qb = Q_BLOCK if Tq % Q_BLOCK == 0 else Tq
    nb = Tq // qb
    qs = jnp.moveaxis(q.reshape(B, nb, qb, G, R, d), 1, 0)
    iqs = jnp.moveaxis(iq.reshape(B, nb, qb, IDX_HEADS, IDX_DIM), 1, 0)
    iws = jnp.moveaxis(iw.reshape(B, nb, qb, IDX_HEADS), 1, 0)
    pb = q_pos.reshape(nb, qb)

    def block(args):
        q_blk, iq_blk, iw_blk, p_blk = args
        sc = jnp.einsum('bqhd,bkd->bqhk', iq_blk, ik, preferred_element_type=jnp.float32) * IDX_SCALE
        score = jnp.einsum('bqhk,bqh->bqk', jax.nn.relu(sc), iw_blk.astype(jnp.float32))
        score = jnp.where((k_pos[None, :] <= p_blk[:, None])[None], score, -jnp.inf)
        top_s, sel = lax.top_k(score, topk)
        valid = top_s > -jnp.inf
        k_sel, v_sel = fetch(sel)
        s = jnp.einsum('bqgrd,bqkgd->bgrqk', q_blk, k_sel, preferred_element_type=jnp.float32) * DSA_SCALE
        s = jnp.where(valid[:, None, None], s, -jnp.inf)
        p = jax.nn.softmax(s, axis=-1).astype(v_sel.dtype)
        return jnp.einsum('bgrqk,bqkgd->bqgrd', p, v_sel)

    out = lax.map(block, (qs, iqs, iws, pb))
    return jnp.moveaxis(out, 0, 1).reshape(B, Tq, H, d)


def make_paged_fetch(cache_k, cache_v, page_table, layer, k_new, v_new, past_len):
    n_new = k_new.shape[1]

    def fetch(sel):
        past = jnp.minimum(sel, past_len - 1)
        phys = jax.vmap(lambda pt, i: pt[i])(page_table, past // PAGE_SIZE)
        slot = past % PAGE_SIZE
        is_new = (sel >= past_len)[..., None, None]
        new_i = jnp.clip(sel - past_len, 0, n_new - 1)
        k = jnp.where(is_new, gather_rows(k_new, new_i), cache_k[phys, layer, slot])
        v = jnp.where(is_new, gather_rows(v_new, new_i), cache_v[phys, layer, slot])
        return k, v
    return fetch


def memory_attention(q, mk, mv):
    s = jnp.einsum('bqhd,bkhd->bhqk', q, mk, preferred_element_type=jnp.float32) * MEM_SCALE
    p = jax.nn.softmax(s, axis=-1).astype(mv.dtype)
    return jnp.einsum('bhqk,bkhd->bqhd', p, mv)


def finish(x, t, outs, w_branch_l, w_out_l, g, b):
    B, T, _ = x.shape
    zs = (t['fox_z'], t['mla_z'], t['dsa_z'], t['mem_z'])
    gates = jax.nn.sigmoid(t['gates'].reshape(B, T, N_BRANCH, D_MODEL))
    offs = np.cumsum((0,) + BRANCH_WIDTHS).tolist()
    y = jnp.zeros_like(x)
    for i in range(N_BRANCH):
        u = outs[i].reshape(B, T, -1) * jax.nn.silu(zs[i])
        y = y + gates[:, :, i] * jnp.einsum('btw,wd->btd', u, w_branch_l[offs[i]:offs[i + 1]])
    out = jnp.einsum('btd,de->bte', y, w_out_l)
    return layer_norm(ALPHA * x + out, g, b)


def setup_inputs(seed: int = 0) -> dict:
    key = jax.random.key(seed)
    ks = jax.random.split(key, 32)
    n_pages = PAST_LEN // PAGE_SIZE
    n_used = DEC_BATCH * n_pages
    n_pool = n_used + n_used // 4
    nrm = lambda k, shape, s=1.0: jax.random.normal(k, shape, jnp.float32) * s
    pg = lambda k, *tail: nrm(k, (n_pool, DEPTH, PAGE_SIZE) + tail)
    page_table = jax.random.permutation(ks[0], n_pool)[:n_used].reshape(DEC_BATCH, n_pages).astype(jnp.int32)

    cuts = np.cumsum((0,) + IN_SPLITS)
    col_scale = np.ones(int(cuts[-1]), np.float32)
    for name in ('fox_v', 'dsa_v'):
        i = IN_NAMES.index(name)
        col_scale[cuts[i]:cuts[i + 1]] = BETA
    kv_up_scale = np.concatenate([np.ones(MLA_NOPE, np.float32), np.full(MLA_V, BETA, np.float32)])
    mem_kv_scale = np.concatenate([np.ones(MEM_W, np.float32), np.full(MEM_W, BETA, np.float32)])
    row_scale = np.concatenate([np.full(w, w ** -0.5, np.float32) for w in BRANCH_WIDTHS])

    return {
        'x_prompt': nrm(ks[1], (BATCH, SEQ, D_MODEL)),
        'x_sample': nrm(ks[2], (DEC_BATCH, DEC_SEQ, D_MODEL)),
        'cache_fox_k': pg(ks[3], FOX_KV_HEADS, HEAD_DIM),
        'cache_fox_v': pg(ks[4], FOX_KV_HEADS, HEAD_DIM),
        'cache_fox_logf': jax.nn.log_sigmoid(FORGET_BIAS_INIT + pg(ks[5], FOX_HEADS)),
        'cache_mla_ckv': pg(ks[6], MLA_KV_LORA),
        'cache_mla_krope': pg(ks[7], MLA_ROPE),
        'cache_dsa_k': pg(ks[8], DSA_KV_HEADS, HEAD_DIM),
        'cache_dsa_v': pg(ks[9], DSA_KV_HEADS, HEAD_DIM),
        'cache_dsa_kidx': pg(ks[10], IDX_DIM),
        'cache_mem_k': nrm(ks[11], (DEC_BATCH, DEPTH, MEM_LEN, MEM_HEADS, HEAD_DIM)),
        'cache_mem_v': nrm(ks[12], (DEC_BATCH, DEPTH, MEM_LEN, MEM_HEADS, HEAD_DIM)),
        'page_table': page_table,
        'mem_prompt': nrm(ks[13], (BATCH, MEM_LEN, D_MODEL)),
        'w_in': nrm(ks[14], (DEPTH, D_MODEL, IN_WIDTH), D_MODEL ** -0.5) * jnp.asarray(col_scale),
        'b_forget': FORGET_BIAS_INIT + nrm(ks[15], (DEPTH, FOX_HEADS), 0.5),
        'mla_q_norm': 1.0 + nrm(ks[16], (DEPTH, MLA_Q_LORA), 0.05),
        'mla_kv_norm': 1.0 + nrm(ks[17], (DEPTH, MLA_KV_LORA), 0.05),
        'w_mla_q_up': nrm(ks[18], (DEPTH, MLA_Q_LORA, MLA_HEADS * (MLA_NOPE + MLA_ROPE)), MLA_Q_LORA ** -0.5),
        'w_mla_kv_up': nrm(ks[19], (DEPTH, MLA_KV_LORA, MLA_HEADS, MLA_NOPE + MLA_V), MLA_KV_LORA ** -0.5) * jnp.asarray(kv_up_scale),
        'w_mem_kv': nrm(ks[20], (DEPTH, D_MODEL, 2 * MEM_W), D_MODEL ** -0.5) * jnp.asarray(mem_kv_scale),
        'w_branch': nrm(ks[21], (DEPTH, BRANCH_TOTAL, D_MODEL)) * jnp.asarray(row_scale)[:, None],
        'w_out': nrm(ks[22], (DEPTH, D_MODEL, D_MODEL), D_MODEL ** -0.5 * BETA),
        'ln_gain': 1.0 + nrm(ks[23], (DEPTH, D_MODEL), 0.05),
        'ln_bias': nrm(ks[24], (DEPTH, D_MODEL), 0.02),
    }


def reference(x_prompt, x_sample, cache_fox_k, cache_fox_v, cache_fox_logf, cache_mla_ckv, cache_mla_krope,
              cache_dsa_k, cache_dsa_v, cache_dsa_kidx, cache_mem_k, cache_mem_v, page_table, mem_prompt,
              w_in, b_forget, mla_q_norm, mla_kv_norm, w_mla_q_up, w_mla_kv_up, w_mem_kv, w_branch, w_out,
              ln_gain, ln_bias):
    B, T, _ = x_prompt.shape
    Bs, Ts, _ = x_sample.shape
    M = mem_prompt.shape[1]
    past_len = page_table.shape[1] * PAGE_SIZE
    pos_p = jnp.arange(T, dtype=jnp.int32)
    pos_s = past_len + jnp.arange(Ts, dtype=jnp.int32)
    pos_all = jnp.arange(past_len + Ts, dtype=jnp.int32)

    def past_rows(cache, layer):
        g = cache[page_table, layer]
        return g.reshape((Bs, past_len) + g.shape[3:])

    def with_new(cache, layer, new):
        return jnp.concatenate([past_rows(cache, layer).astype(new.dtype), new], axis=1)

    row_names = ('fox_k', 'fox_v', 'fox_logf', 'mla_ckv', 'mla_krope', 'dsa_k', 'dsa_v', 'idx_k')
    P = {n: [] for n in row_names + ('mem_k', 'mem_v')}
    S = {n: [] for n in row_names}
    xp, xs = x_prompt, x_sample
    for l in range(DEPTH):
        proj = (w_in[l], b_forget[l], mla_q_norm[l], mla_kv_norm[l], w_mla_q_up[l], w_mla_kv_up[l])

        tp = project(xp, pos_p, *proj)
        c_p = jnp.cumsum(tp['fox_logf'], axis=1)
        fox_o = blocked_attention([(tp['fox_q'], tp['fox_k'])], tp['fox_v'], pos_p, pos_p, FOX_SCALE, c_p, c_p)
        mla_o = mla_attend(tp['mla_qlat'], tp['mla_qrope'], tp['mla_ckv'], tp['mla_krope'], pos_p, pos_p, w_mla_kv_up[l])
        dk_p, dv_p = tp['dsa_k'], tp['dsa_v']
        fetch_p = lambda sel, dk_p=dk_p, dv_p=dv_p: (gather_rows(dk_p, sel), gather_rows(dv_p, sel))
        dsa_o = dsa_attention(tp['dsa_q'], tp['idx_q'], tp['idx_w'], tp['idx_k'], pos_p, pos_p, fetch_p)
        mkv = jnp.einsum('bmd,de->bme', mem_prompt, w_mem_kv[l])
        mk = mkv[..., :MEM_W].reshape(B, M, MEM_HEADS, HEAD_DIM)
        mv = mkv[..., MEM_W:].reshape(B, M, MEM_HEADS, HEAD_DIM)
        mem_o = memory_attention(tp['mem_q'], mk, mv)
        y_p = finish(xp, tp, (fox_o, mla_o, dsa_o, mem_o), w_branch[l], w_out[l], ln_gain[l], ln_bias[l])

        ts = project(xs, pos_s, *proj)
        fk = with_new(cache_fox_k, l, ts['fox_k'])
        fv = with_new(cache_fox_v, l, ts['fox_v'])
        c_s = jnp.cumsum(with_new(cache_fox_logf, l, ts['fox_logf']).astype(jnp.float32), axis=1)
        fox_o_s = blocked_attention([(ts['fox_q'], fk)], fv, pos_s, pos_all, FOX_SCALE, c_s[:, past_len:], c_s)
        ckv = with_new(cache_mla_ckv, l, ts['mla_ckv'])
        kr = with_new(cache_mla_krope, l, ts['mla_krope'])
        mla_o_s = mla_attend(ts['mla_qlat'], ts['mla_qrope'], ckv, kr, pos_s, pos_all, w_mla_kv_up[l])
        ik = with_new(cache_dsa_kidx, l, ts['idx_k'])
        fetch_s = make_paged_fetch(cache_dsa_k, cache_dsa_v, page_table, l, ts['dsa_k'], ts['dsa_v'], past_len)
        dsa_o_s = dsa_attention(ts['dsa_q'], ts['idx_q'], ts['idx_w'], ik, pos_s, pos_all, fetch_s)
        mem_o_s = memory_attention(ts['mem_q'], cache_mem_k[:, l], cache_mem_v[:, l])
        y_s = finish(xs, ts, (fox_o_s, mla_o_s, dsa_o_s, mem_o_s), w_branch[l], w_out[l], ln_gain[l], ln_bias[l])

        for n in row_names:
            P[n].append(tp[n])
            S[n].append(ts[n])
        P['mem_k'].append(mk)
        P['mem_v'].append(mv)
        xp, xs = y_p, y_s

    sp = {n: jnp.stack(v, axis=1) for n, v in P.items()}
    ss = {n: jnp.stack(v, axis=1) for n, v in S.items()}
    return (xp, xs,
            sp['fox_k'], sp['fox_v'], sp['fox_logf'], sp['mla_ckv'], sp['mla_krope'], sp['dsa_k'], sp['dsa_v'], sp['idx_k'],
            sp['mem_k'], sp['mem_v'],
            ss['fox_k'], ss['fox_v'], ss['fox_logf'], ss['mla_ckv'], ss['mla_krope'], ss['dsa_k'], ss['dsa_v'], ss['idx_k'])
```

```python
import functools
import math

import numpy as np
import jax
import jax.numpy as jnp
from jax import lax
from jax.experimental import pallas as pl
from jax.experimental.pallas import tpu as pltpu

F32 = jnp.float32
BF16 = jnp.bfloat16
I32 = jnp.int32

HEAD_DIM = 64
ROT_DIM = HEAD_DIM // 4
ROPE_THETA = 500000.0
FOX_HEADS = 8
FOX_KV_HEADS = 2
MLA_HEADS = 8
MLA_NOPE = 64
MLA_ROPE = 32
MLA_V = 64
MLA_Q_LORA = 384
MLA_KV_LORA = 256
DSA_HEADS = 8
DSA_KV_HEADS = 2
IDX_HEADS = 8
IDX_DIM = 64
TOPK_MAX = 256
MEM_HEADS = 4
N_BRANCH = 4
PAGE_SIZE = 128
FOX_W = FOX_HEADS * HEAD_DIM
MLA_W = MLA_HEADS * MLA_V
DSA_W = DSA_HEADS * HEAD_DIM
MEM_W = MEM_HEADS * HEAD_DIM
BRANCH_WIDTHS = (FOX_W, MLA_W, DSA_W, MEM_W)
FOX_SCALE = HEAD_DIM ** -0.5
MLA_SCALE = (MLA_NOPE + MLA_ROPE) ** -0.5
DSA_SCALE = HEAD_DIM ** -0.5
MEM_SCALE = HEAD_DIM ** -0.5
IDX_SCALE = IDX_DIM ** -0.5
IDX_W_SCALE = IDX_HEADS ** -0.5
LN_EPS = 1e-5
RMS_EPS = 1e-6

LANES = 128
VMEM_LIMIT_BYTES = 48 * 2 ** 20

NEG = -1e30
INT_MIN = -2 ** 31

_GROUPS = (
    ('gates', 4096, 4096),
    ('fox_q', 512, 512), ('fox_z', 512, 512), ('mla_z', 512, 512), ('dsa_q', 512, 512),
    ('dsa_z', 512, 512), ('idx_q', 512, 512), ('mla_qa', 384, 512),
    ('mla_kva', 256, 256), ('mem_q', 256, 256), ('mem_z', 256, 256),
    ('fox_k', 128, 128), ('fox_v', 128, 128), ('dsa_k', 128, 128), ('dsa_v', 128, 128),
    ('idx_k', 64, 128), ('mla_kr', 32, 128), ('fox_f', 8, 128), ('idx_w', 8, 128),
)
_IN_NAMES = ('fox_q', 'fox_k', 'fox_v', 'fox_f', 'fox_z', 'mla_qa', 'mla_kva', 'mla_kr', 'mla_z',
             'dsa_q', 'dsa_k', 'dsa_v', 'idx_q', 'idx_k', 'idx_w', 'dsa_z', 'mem_q', 'mem_z', 'gates')
_IN_SPLITS = (512, 128, 128, 8, 512, 384, 256, 32, 512, 512, 128, 128, 512, 64, 8, 512, 256, 256, 4096)


def _layout():
    off, out = 0, {}
    for name, real, pad in _GROUPS:
        assert off % pad == 0
        out[name] = (off, real, pad)
        off += pad
    return out, off


_LAYOUT, EP = _layout()
_SRC = dict(zip(_IN_NAMES, np.cumsum((0,) + _IN_SPLITS[:-1]).tolist()))


def _blk(name):
    off, _, pad = _LAYOUT[name]
    return off // pad


def _cparams(n_grid):
    return pltpu.CompilerParams(dimension_semantics=('arbitrary',) * n_grid, vmem_limit_bytes=VMEM_LIMIT_BYTES)


def _dot(a, b):
    return jnp.dot(a, b, preferred_element_type=F32)


def _dot_nt(a, b):
    return lax.dot_general(a, b, (((1,), (1,)), ((), ())), preferred_element_type=F32)


def _dot_tn(a, b):
    return lax.dot_general(a, b, (((0,), (0,)), ((), ())), preferred_element_type=F32)


def _split3(x):
    hi = x.astype(BF16)
    r = x - hi.astype(F32)
    mid = r.astype(BF16)
    lo = (r - mid.astype(F32)).astype(BF16)
    return hi, mid, lo


def _iota(shape, dim):
    return lax.broadcasted_iota(I32, shape, dim)


def _mm_kernel(x_ref, w_ref, o_ref):
    o_ref[...] = _dot(x_ref[...].astype(BF16), w_ref[...].astype(BF16))


def _matmul(x, w, tm, tn):
    m, k = x.shape
    n = w.shape[1]
    assert m % tm == 0 and n % tn == 0
    return pl.pallas_call(
        _mm_kernel,
        grid=(m // tm, n // tn),
        in_specs=[pl.BlockSpec((tm, k), lambda i, j: (i, 0)), pl.BlockSpec((k, tn), lambda i, j: (0, j))],
        out_specs=pl.BlockSpec((tm, tn), lambda i, j: (i, j)),
        out_shape=jax.ShapeDtypeStruct((m, n), F32),
        compiler_params=_cparams(2),
        name='proj_matmul',
    )(x, w)


def _rope(x, cos, sin, half, width):
    w = x.shape[-1]
    d = _iota(x.shape, 1) % width
    up = pltpu.roll(x, w - half, 1)
    dn = pltpu.roll(x, half, 1)
    partner = jnp.where(d < half, up, dn)
    return jnp.where(d < 2 * half, x * cos + partner * sin, x)


def _tile_lanes(x, n):
    return jnp.concatenate([x] * n, axis=-1) if n > 1 else x


def _prep_kernel(qa_ref, kva_ref, dq_ref, dk_ref, iq_ref, ik_ref, kr_ref, ff_ref, iw_ref,
                 ca_ref, sa_ref, cb_ref, sb_ref, bf_ref, qn_ref, kvn_ref, wqn_ref, wqr_ref, wkt_ref,
                 logf_ref, ckv_ref, kr4_ref, dqo_ref, dko_ref, iqo_ref, ik2_ref, iwo_ref, qlat_ref, qrope_ref):
    ca, sa, cb, sb = ca_ref[...], sa_ref[...], cb_ref[...], sb_ref[...]
    half_a, half_b = ROT_DIM // 2, MLA_ROPE // 2
    logf_ref[...] = jax.nn.log_sigmoid(ff_ref[...] + bf_ref[...])[:, :FOX_HEADS]
    kva = kva_ref[...]
    ckv_ref[...] = kva * lax.rsqrt(jnp.mean(kva * kva, axis=-1, keepdims=True) + RMS_EPS) * kvn_ref[...]
    kr = _rope(kr_ref[...], cb, sb, half_b, MLA_ROPE)
    kr4_ref[...] = kr + pltpu.roll(kr, 32, 1) + pltpu.roll(kr, 64, 1) + pltpu.roll(kr, 96, 1)
    dqo_ref[...] = _rope(dq_ref[...], _tile_lanes(ca, 4), _tile_lanes(sa, 4), half_a, HEAD_DIM)
    dko_ref[...] = _rope(dk_ref[...], ca, sa, half_a, HEAD_DIM)
    iqo_ref[...] = _rope(iq_ref[...], _tile_lanes(ca, 4), _tile_lanes(sa, 4), half_a, IDX_DIM)
    ik = _rope(ik_ref[...], ca, sa, half_a, IDX_DIM)
    ik2_ref[...] = ik + pltpu.roll(ik, 64, 1)
    iwo_ref[...] = (iw_ref[...] * IDX_W_SCALE)[:, :IDX_HEADS]
    qa = qa_ref[...]
    ms = jnp.sum(qa * qa, axis=-1, keepdims=True) * (1.0 / MLA_Q_LORA)
    qan = (qa * lax.rsqrt(ms + RMS_EPS) * qn_ref[...]).astype(BF16)
    q_nope = _dot(qan, wqn_ref[...])
    q_rope = _dot(qan, wqr_ref[...])
    qrope_ref[...] = _rope(q_rope, _tile_lanes(cb, 2), _tile_lanes(sb, 2), half_b, MLA_ROPE)
    half = _iota((qa.shape[0], LANES), 1) // HEAD_DIM
    for h in range(MLA_HEADS):
        m, s = divmod(h, 2)
        blk = q_nope[:, LANES * m:LANES * (m + 1)]
        masked = jnp.where(half == s, blk, 0.0).astype(BF16)
        qlat_ref[:, MLA_KV_LORA * h:MLA_KV_LORA * (h + 1)] = _dot(masked, wkt_ref[LANES * m:LANES * (m + 1), :])


def _prep(h2, tabs, params, tr):
    n = h2.shape[0]
    nt = tabs[0].shape[0] // tr
    row = lambda w, b: pl.BlockSpec((tr, w), lambda i, b=b: (i, b))
    tab = pl.BlockSpec((tr, LANES), lambda i: (i % nt, 0))
    full = lambda a: pl.BlockSpec(a.shape, lambda i: (0,) * a.ndim)
    in_specs = [row(512, _blk('mla_qa')), row(256, _blk('mla_kva')), row(512, _blk('dsa_q')), row(128, _blk('dsa_k')),
                row(512, _blk('idx_q')), row(128, _blk('idx_k')), row(128, _blk('mla_kr')), row(128, _blk('fox_f')),
                row(128, _blk('idx_w'))] + [tab] * 4 + [full(p) for p in params]
    outs = [('logf', FOX_HEADS), ('ckv', MLA_KV_LORA), ('kr4', LANES), ('dq', DSA_W), ('dk', LANES), ('iq', 512),
            ('ik2', LANES), ('iw', IDX_HEADS), ('qlat', MLA_HEADS * MLA_KV_LORA), ('qrope', MLA_HEADS * MLA_ROPE)]
    res = pl.pallas_call(
        _prep_kernel,
        grid=(n // tr,),
        in_specs=in_specs,
        out_specs=[pl.BlockSpec((tr, w), lambda i: (i, 0)) for _, w in outs],
        out_shape=[jax.ShapeDtypeStruct((n, w), F32) for _, w in outs],
        compiler_params=_cparams(1),
        name='prep_rows',
    )(*([h2] * 9), *tabs, *params)
    return dict(zip([k for k, _ in outs], res))


def _cumsum_kernel(lf_ref, c_ref, ct_ref, *, blk):
    t = lf_ref.shape[1]
    r, c = _iota((blk, blk), 0), _iota((blk, blk), 1)
    tri = jnp.where(c <= r, 1.0, 0.0).astype(BF16)
    trit = jnp.where(r <= c, 1.0, 0.0).astype(BF16)
    carry = jnp.zeros((1, FOX_HEADS), F32)
    carry_t = jnp.zeros((FOX_HEADS, 1), F32)
    for i in range(t // blk):
        parts = _split3(lf_ref[0, blk * i:blk * (i + 1), :])
        cb = sum(_dot(tri, p) for p in parts) + carry
        cbt = sum(_dot_tn(p, trit) for p in parts) + carry_t
        c_ref[0, blk * i:blk * (i + 1), :] = cb
        ct_ref[0, :, blk * i:blk * (i + 1)] = cbt
        carry = cb[blk - 1:blk, :]
        carry_t = cbt[:, blk - 1:blk]


def _cumsum(lf):
    b, t, h = lf.shape
    return pl.pallas_call(
        functools.partial(_cumsum_kernel, blk=min(256, t)),
        grid=(b,),
        in_specs=[pl.BlockSpec((1, t, h), lambda i: (i, 0, 0))],
        out_specs=[pl.BlockSpec((1, t, h), lambda i: (i, 0, 0)), pl.BlockSpec((1, h, t), lambda i: (i, 0, 0))],
        out_shape=[jax.ShapeDtypeStruct((b, t, h), F32), jax.ShapeDtypeStruct((b, h, t), F32)],
        compiler_params=_cparams(1),
        name='forget_cumsum',
    )(lf)


def _slot(x, g, s):
    half = _iota(x.shape, 1) // HEAD_DIM
    src = x if g == s else pltpu.roll(x, HEAD_DIM, 1)
    return jnp.where(half == s, src, 0.0)


def _online(state, s, v):
    m_i, l_i, acc = state
    m_new = jnp.maximum(m_i, jnp.max(s, axis=-1, keepdims=True))
    alpha = jnp.exp(m_i - m_new)
    p = jnp.exp(s - m_new)
    l_new = alpha * l_i + jnp.sum(p, axis=-1, keepdims=True)
    acc_new = alpha * acc + _dot(p.astype(BF16), v)
    return m_new, l_new, acc_new


def _init_state(tq, dv):
    return (jnp.full((tq, 1), NEG, F32), jnp.zeros((tq, 1), F32), jnp.zeros((tq, dv), F32))


def _causal(s, i, j, tq, tk):
    qpos = i * tq + _iota(s.shape, 0)
    kpos = j * tk + _iota(s.shape, 1)
    return jnp.where(kpos <= qpos, s, NEG)


def _fox_p_kernel(q_ref, k_ref, v_ref, c_ref, ct_ref, o_ref, *, tq, nc):
    i = pl.program_id(1)
    for m in range(FOX_HEADS // 2):
        qp = q_ref[0, :, LANES * m:LANES * (m + 1)].astype(BF16)
        pair = jnp.zeros((tq, LANES), F32)
        for s in range(2):
            h = 2 * m + s
            g = h // (FOX_HEADS // FOX_KV_HEADS)
            cq = c_ref[0, :, h:h + 1]

            def body(j, st, g=g, s=s, h=h, cq=cq, qp=qp):
                kc = _slot(k_ref[0, pl.ds(j * tq, tq), :], g, s).astype(BF16)
                vc = _slot(v_ref[0, pl.ds(j * tq, tq), :], g, s).astype(BF16)
                ck = ct_ref[0, pl.ds(h * nc + j, 1), :]
                sc = _dot_nt(qp, kc) * FOX_SCALE + (cq - ck)
                return _online(st, _causal(sc, i, j, tq, tq), vc)

            _, l_i, acc = lax.fori_loop(0, i + 1, body, _init_state(tq, LANES))
            pair = pair + acc / l_i
        o_ref[0, :, LANES * m:LANES * (m + 1)] = pair


def _fox_prompt(h3, c, ct, tq):
    b, t, _ = h3.shape
    nc = t // tq
    ctr = ct.reshape(b, FOX_HEADS * nc, tq)
    return pl.pallas_call(
        functools.partial(_fox_p_kernel, tq=tq, nc=nc),
        grid=(b, nc),
        in_specs=[pl.BlockSpec((1, tq, 512), lambda bi, i: (bi, i, _blk('fox_q'))),
                  pl.BlockSpec((1, t, LANES), lambda bi, i: (bi, 0, _blk('fox_k'))),
                  pl.BlockSpec((1, t, LANES), lambda bi, i: (bi, 0, _blk('fox_v'))),
                  pl.BlockSpec((1, tq, FOX_HEADS), lambda bi, i: (bi, i, 0)),
                  pl.BlockSpec((1, FOX_HEADS * nc, tq), lambda bi, i: (bi, 0, 0))],
        out_specs=pl.BlockSpec((1, tq, FOX_W), lambda bi, i: (bi, i, 0)),
        out_shape=jax.ShapeDtypeStruct((b, t, FOX_W), F32),
        compiler_params=_cparams(2),
        name='fox_prompt',
    )(h3, h3, h3, c, ctr)


def _mla_p_kernel(ql_ref, qr_ref, ckv_ref, kr_ref, wv_ref, o_ref, *, tq):
    i = pl.program_id(1)
    slot4 = _iota((tq, LANES), 1) // MLA_ROPE
    for m in range(MLA_HEADS // 2):
        pair = jnp.zeros((tq, LANES), F32)
        for s in range(2):
            h = 2 * m + s
            ql = ql_ref[0, :, MLA_KV_LORA * h:MLA_KV_LORA * (h + 1)].astype(BF16)
            qr = jnp.where(slot4 == h % 4, qr_ref[0, :, LANES * (h // 4):LANES * (h // 4 + 1)], 0.0).astype(BF16)

            def body(j, st, ql=ql, qr=qr):
                kc = ckv_ref[0, pl.ds(j * tq, tq), :].astype(BF16)
                kr = kr_ref[0, pl.ds(j * tq, tq), :].astype(BF16)
                sc = (_dot_nt(ql, kc) + _dot_nt(qr, kr)) * MLA_SCALE
                return _online(st, _causal(sc, i, j, tq, tq), kc)

            _, l_i, acc = lax.fori_loop(0, i + 1, body, _init_state(tq, MLA_KV_LORA))
            pair = pair + _dot((acc / l_i).astype(BF16), wv_ref[h])
        o_ref[0, :, LANES * m:LANES * (m + 1)] = pair


def _mla_prompt(qlat, qrope, ckv, kr4, wv_slot, tq):
    b, t, _ = qlat.shape
    return pl.pallas_call(
        functools.partial(_mla_p_kernel, tq=tq),
        grid=(b, t // tq),
        in_specs=[pl.BlockSpec((1, tq, qlat.shape[2]), lambda bi, i: (bi, i, 0)),
                  pl.BlockSpec((1, tq, qrope.shape[2]), lambda bi, i: (bi, i, 0)),
                  pl.BlockSpec((1, t, MLA_KV_LORA), lambda bi, i: (bi, 0, 0)),
                  pl.BlockSpec((1, t, LANES), lambda bi, i: (bi, 0, 0)),
                  pl.BlockSpec(wv_slot.shape, lambda bi, i: (0, 0, 0))],
        out_specs=pl.BlockSpec((1, tq, MLA_W), lambda bi, i: (bi, i, 0)),
        out_shape=jax.ShapeDtypeStruct((b, t, MLA_W), F32),
        compiler_params=_cparams(2),
        name='mla_prompt',
    )(qlat, qrope, ckv, kr4, wv_slot)


def _sort_key(score):
    b = pltpu.bitcast(jnp.where(score == 0.0, 0.0, score), I32)
    return b ^ ((b >> 31) & 0x7FFFFFFF)


def _select_bias(keys_ref, bias_ref, topk):
    nc, rows, tk = keys_ref.shape
    kf = float(topk)

    def count(pred):
        return jnp.sum(jnp.sum(jnp.where(pred, 1.0, 0.0), axis=0), axis=-1, keepdims=True)

    cur = jnp.where(count(keys_ref[...] >= 0) >= kf, 0, INT_MIN).astype(I32)

    def vbody(b, cur):
        cand = cur | jnp.left_shift(jnp.int32(1), 30 - b)
        return jnp.where(count(keys_ref[...] >= cand[None]) >= kf, cand, cur)

    thr = lax.fori_loop(0, 31, vbody, cur)
    need = kf - count(keys_ref[...] > thr[None])
    idx = _iota((nc, rows, tk), 0) * tk + _iota((nc, rows, tk), 2)
    nbits = max(1, int(math.ceil(math.log2(nc * tk))))

    def ibody(b, p):
        cand = p + jnp.left_shift(jnp.int32(1), nbits - 1 - b)
        x = keys_ref[...]
        return jnp.where(count((x == thr[None]) & (idx < cand[None])) < need, cand, p)

    p = lax.fori_loop(0, nbits, ibody, jnp.zeros((rows, 1), I32))
    x = keys_ref[...]
    sel = (x > INT_MIN) & ((x > thr[None]) | ((x == thr[None]) & (idx <= p[None])))
    bias_ref[...] = jnp.where(sel, 0.0, NEG)


def _dsa_p_kernel(iq_ref, iw_ref, ik_ref, q_ref, k_ref, v_ref, o_ref, keys_ref, bias_ref, *, tq, topk):
    i = pl.program_id(1)
    half = _iota((tq, LANES), 1) // IDX_DIM
    keys_ref[...] = jnp.full(keys_ref.shape, INT_MIN, I32)
    iqs = [jnp.where(half == h % 2, iq_ref[0, :, LANES * (h // 2):LANES * (h // 2 + 1)], 0.0).astype(BF16)
           for h in range(IDX_HEADS)]
    iw = iw_ref[0]

    def ibody(j, carry):
        ik = ik_ref[0, pl.ds(j * tq, tq), :].astype(BF16)
        score = jnp.zeros((tq, tq), F32)
        for h in range(IDX_HEADS):
            sc = _dot_nt(iqs[h], ik) * IDX_SCALE
            score = score + jnp.maximum(sc, 0.0) * iw[:, h:h + 1]
        qpos = i * tq + _iota((tq, tq), 0)
        kpos = j * tq + _iota((tq, tq), 1)
        keys_ref[j] = jnp.where(kpos <= qpos, _sort_key(score), INT_MIN)
        return carry

    lax.fori_loop(0, i + 1, ibody, 0)
    _select_bias(keys_ref, bias_ref, topk)

    for m in range(DSA_HEADS // 2):
        qp = q_ref[0, :, LANES * m:LANES * (m + 1)].astype(BF16)
        pair = jnp.zeros((tq, LANES), F32)
        for s in range(2):
            g = (2 * m + s) // (DSA_HEADS // DSA_KV_HEADS)

            def body(j, st, g=g, s=s, qp=qp):
                kc = _slot(k_ref[0, pl.ds(j * tq, tq), :], g, s).astype(BF16)
                vc = _slot(v_ref[0, pl.ds(j * tq, tq), :], g, s).astype(BF16)
                sc = _dot_nt(qp, kc) * DSA_SCALE + bias_ref[j]
                return _online(st, sc, vc)

            _, l_i, acc = lax.fori_loop(0, i + 1, body, _init_state(tq, LANES))
            pair = pair + acc / l_i
        o_ref[0, :, LANES * m:LANES * (m + 1)] = pair


def _dsa_prompt(iq, iw, ik2, dq, dk, h3, tq):
    b, t, _ = dq.shape
    nc = t // tq
    topk = min(TOPK_MAX, t // 4)
    return pl.pallas_call(
        functools.partial(_dsa_p_kernel, tq=tq, topk=topk),
        grid=(b, nc),
        in_specs=[pl.BlockSpec((1, tq, 512), lambda bi, i: (bi, i, 0)),
                  pl.BlockSpec((1, tq, IDX_HEADS), lambda bi, i: (bi, i, 0)),
                  pl.BlockSpec((1, t, LANES), lambda bi, i: (bi, 0, 0)),
                  pl.BlockSpec((1, tq, DSA_W), lambda bi, i: (bi, i, 0)),
                  pl.BlockSpec((1, t, LANES), lambda bi, i: (bi, 0, 0)),
                  pl.BlockSpec((1, t, LANES), lambda bi, i: (bi, 0, _blk('dsa_v')))],
        out_specs=pl.BlockSpec((1, tq, DSA_W), lambda bi, i: (bi, i, 0)),
        out_shape=jax.ShapeDtypeStruct((b, t, DSA_W), F32),
        scratch_shapes=[pltpu.VMEM((nc, tq, tq), I32), pltpu.VMEM((nc, tq, tq), F32)],
        compiler_params=_cparams(2),
        name='dsa_prompt',
    )(iq, iw, ik2, dq, dk, h3)


def _mem_kernel(q_ref, k_ref, v_ref, o_ref):
    tq = q_ref.shape[1]
    half = _iota((k_ref.shape[-2], LANES), 1) // HEAD_DIM
    for m in range(MEM_HEADS // 2):
        qp = q_ref[0, :, LANES * m:LANES * (m + 1)].astype(BF16)
        kp = k_ref[:, LANES * m:LANES * (m + 1)] if k_ref.ndim == 2 else k_ref[0, :, LANES * m:LANES * (m + 1)]
        vp = v_ref[:, LANES * m:LANES * (m + 1)] if v_ref.ndim == 2 else v_ref[0, :, LANES * m:LANES * (m + 1)]
        pair = jnp.zeros((tq, LANES), F32)
        for s in range(2):
            kc = jnp.where(half == s, kp, 0.0).astype(BF16)
            vc = jnp.where(half == s, vp, 0.0).astype(BF16)
            sc = _dot_nt(qp, kc) * MEM_SCALE
            p = jnp.exp(sc - jnp.max(sc, axis=-1, keepdims=True))
            p = p / jnp.sum(p, axis=-1, keepdims=True)
            pair = pair + _dot(p.astype(BF16), vc)
        o_ref[0, :, LANES * m:LANES * (m + 1)] = pair


def _mem_attention(q_arr, q_spec, k_arr, k_spec, v_arr, v_spec, grid, out_shape, out_spec):
    return pl.pallas_call(
        _mem_kernel, grid=grid, in_specs=[q_spec, k_spec, v_spec], out_specs=out_spec,
        out_shape=jax.ShapeDtypeStruct(out_shape, F32), compiler_params=_cparams(len(grid)), name='mem_attention',
    )(q_arr, k_arr, v_arr)


def _finish_kernel(x_ref, fo_ref, mo_ref, do_ref, eo_ref, fz_ref, mz_ref, dz_ref, ez_ref, gt_ref,
                   wb_ref, wo_ref, g_ref, b_ref, o_ref, *, alpha):
    offs = np.cumsum((0,) + BRANCH_WIDTHS).tolist()
    d = x_ref.shape[-1]
    y = jnp.zeros(x_ref.shape, F32)
    for n, (o_r, z_r) in enumerate(((fo_ref, fz_ref), (mo_ref, mz_ref), (do_ref, dz_ref), (eo_ref, ez_ref))):
        z = z_r[...]
        u = o_r[...] * (z * jax.nn.sigmoid(z))
        pb = _dot(u.astype(BF16), wb_ref[offs[n]:offs[n + 1], :])
        y = y + jax.nn.sigmoid(gt_ref[:, d * n:d * (n + 1)]) * pb
    r = alpha * x_ref[...] + _dot(y.astype(BF16), wo_ref[...])
    mu = jnp.mean(r, axis=-1, keepdims=True)
    var = jnp.mean(jnp.square(r - mu), axis=-1, keepdims=True)
    o_ref[...] = (r - mu) * lax.rsqrt(var + LN_EPS) * g_ref[...] + b_ref[...]


def _finish(x2, h2, fo, mo, do, eo, wb, wo, g, b, tr, alpha):
    n, d = x2.shape
    row = lambda w: pl.BlockSpec((tr, w), lambda i: (i, 0))
    hcol = lambda name: pl.BlockSpec((tr, _LAYOUT[name][2]), lambda i, name=name: (i, _blk(name)))
    full = lambda a: pl.BlockSpec(a.shape, lambda i: (0,) * a.ndim)
    return pl.pallas_call(
        functools.partial(_finish_kernel, alpha=alpha),
        grid=(n // tr,),
        in_specs=[row(d), row(FOX_W), row(MLA_W), row(DSA_W), row(MEM_W),
                  hcol('fox_z'), hcol('mla_z'), hcol('dsa_z'), hcol('mem_z'), hcol('gates'),
                  full(wb), full(wo), full(g), full(b)],
        out_specs=row(d),
        out_shape=jax.ShapeDtypeStruct((n, d), F32),
        compiler_params=_cparams(1),
        name='finish',
    )(x2, fo, mo, do, eo, h2, h2, h2, h2, h2, wb, wo, g, b)


def _page_specs(n, width, layer, pg, page_of):
    return [pl.BlockSpec((None, None, PAGE_SIZE, width),
                         lambda b, p, pt, j=j: (pt[b, page_of(p, j)], layer, 0, 0)) for j in range(pg)]


def _qbd(q2, rows):
    row = _iota((rows, LANES), 0)
    g_row = (row % FOX_HEADS) // (FOX_HEADS // FOX_KV_HEADS)
    keep = (_iota((rows, LANES), 1) // HEAD_DIM) == g_row
    return keep, jnp.where(keep, q2, 0.0).astype(BF16)


def _fox_s_kernel(pt_ref, q2_ref, lrows_ref, kn_ref, vn_ref, lfn_ref, *rest, pg, n_tok):
    kps, vps, lps = rest[:pg], rest[pg:2 * pg], rest[2 * pg:3 * pg]
    o_ref, m_s, l_s, acc_s, carry_s = rest[3 * pg:]
    p = pl.program_id(1)
    rows = q2_ref.shape[1]
    keep, qbd = _qbd(q2_ref[0], rows)
    lane = _iota((rows, LANES), 1)
    t_row = _iota((rows, LANES), 0) // FOX_HEADS
    sp = _iota((PAGE_SIZE, 2 * PAGE_SIZE), 0)
    cc = _iota((PAGE_SIZE, 2 * PAGE_SIZE), 1)
    msk = jnp.where(((cc < PAGE_SIZE) & (sp >= cc)) | ((cc >= PAGE_SIZE) & (sp > cc - PAGE_SIZE)), 1.0, 0.0).astype(BF16)
    rowc = jnp.sum(jnp.where(lane > t_row, lrows_ref[0], 0.0), axis=-1, keepdims=True)

    def page(k, v, lf, causal):
        sfx = sum(_dot_tn(part, msk) for part in _split3(lf))
        bias8 = sfx[:, PAGE_SIZE:] + carry_s[...]
        carry_s[...] = carry_s[...] + sfx[:, 0:1]
        bias = jnp.concatenate([bias8] * n_tok, axis=0) - rowc
        s = _dot_nt(qbd, k.astype(BF16)) * FOX_SCALE + bias
        if causal:
            s = jnp.where(lane <= t_row, s, NEG)
        m_new, l_new, acc_new = _online((m_s[...], l_s[...], acc_s[...]), s, v.astype(BF16))
        m_s[...], l_s[...], acc_s[...] = m_new, l_new, acc_new

    @pl.when(p == 0)
    def _():
        m_s[...] = jnp.full(m_s.shape, NEG, F32)
        l_s[...] = jnp.zeros(l_s.shape, F32)
        acc_s[...] = jnp.zeros(acc_s.shape, F32)
        carry_s[...] = jnp.zeros(carry_s.shape, F32)
        page(kn_ref[0], vn_ref[0], lfn_ref[0], True)

    for j in range(pg):
        page(kps[j][...], vps[j][...], lps[j][...], False)

    @pl.when(p == pl.num_programs(1) - 1)
    def _():
        o_ref[0] = jnp.where(keep, acc_s[...] / l_s[...], 0.0)


def _fox_sample(pt, q2, lrows, kn, vn, lfn, ck, cv, clf, layer, pg):
    bs, npages = pt.shape
    rows = q2.shape[1]
    rev = lambda p, j: npages - 1 - (p * pg + j)
    per_b = lambda a: pl.BlockSpec((1,) + a.shape[1:], lambda b, p, pt: (b,) + (0,) * (a.ndim - 1))
    grid_spec = pltpu.PrefetchScalarGridSpec(
        num_scalar_prefetch=1, grid=(bs, npages // pg),
        in_specs=[per_b(q2), per_b(lrows), per_b(kn), per_b(vn), per_b(lfn)]
        + _page_specs(ck, LANES, layer, pg, rev) + _page_specs(cv, LANES, layer, pg, rev)
        + _page_specs(clf, FOX_HEADS, layer, pg, rev),
        out_specs=pl.BlockSpec((1, rows, LANES), lambda b, p, pt: (b, 0, 0)),
        scratch_shapes=[pltpu.VMEM((rows, 1), F32), pltpu.VMEM((rows, 1), F32), pltpu.VMEM((rows, LANES), F32),
                        pltpu.VMEM((FOX_HEADS, 1), F32)])
    return pl.pallas_call(
        functools.partial(_fox_s_kernel, pg=pg, n_tok=rows // FOX_HEADS),
        grid_spec=grid_spec, out_shape=jax.ShapeDtypeStruct((bs, rows, LANES), F32),
        compiler_params=_cparams(2), name='fox_sample',
    )(pt, q2, lrows, kn, vn, lfn, *([ck] * pg), *([cv] * pg), *([clf] * pg))


def _mla_s_kernel(pt_ref, q_ref, kn_ref, *rest, pg):
    cps, rps = rest[:pg], rest[pg:2 * pg]
    o_ref, m_s, l_s, acc_s, kc_s = rest[2 * pg:]
    p = pl.program_id(1)
    rows = q_ref.shape[1]
    q = q_ref[0].astype(BF16)
    lane = _iota((rows, PAGE_SIZE), 1)
    t_row = _iota((rows, PAGE_SIZE), 0) // MLA_HEADS

    def page(kc, causal):
        s = _dot_nt(q, kc) * MLA_SCALE
        if causal:
            s = jnp.where(lane <= t_row, s, NEG)
        m_new, l_new, acc_new = _online((m_s[...], l_s[...], acc_s[...]), s, kc[:, :MLA_KV_LORA])
        m_s[...], l_s[...], acc_s[...] = m_new, l_new, acc_new

    @pl.when(p == 0)
    def _():
        m_s[...] = jnp.full(m_s.shape, NEG, F32)
        l_s[...] = jnp.zeros(l_s.shape, F32)
        acc_s[...] = jnp.zeros(acc_s.shape, F32)
        kc_s[...] = jnp.zeros(kc_s.shape, BF16)
        page(kn_ref[0].astype(BF16), True)

    for j in range(pg):
        kc_s[:, :MLA_KV_LORA] = cps[j][...].astype(BF16)
        kc_s[:, MLA_KV_LORA:MLA_KV_LORA + MLA_ROPE] = rps[j][...].astype(BF16)
        page(kc_s[...], False)

    @pl.when(p == pl.num_programs(1) - 1)
    def _():
        o_ref[0] = acc_s[...] / l_s[...]


def _mla_sample(pt, qcat, kn, cckv, ckr, layer, pg):
    bs, npages = pt.shape
    rows, wk = qcat.shape[1], qcat.shape[2]
    fwd = lambda p, j: p * pg + j
    per_b = lambda a: pl.BlockSpec((1,) + a.shape[1:], lambda b, p, pt: (b,) + (0,) * (a.ndim - 1))
    grid_spec = pltpu.PrefetchScalarGridSpec(
        num_scalar_prefetch=1, grid=(bs, npages // pg),
        in_specs=[per_b(qcat), per_b(kn)] + _page_specs(cckv, MLA_KV_LORA, layer, pg, fwd)
        + _page_specs(ckr, MLA_ROPE, layer, pg, fwd),
        out_specs=pl.BlockSpec((1, rows, MLA_KV_LORA), lambda b, p, pt: (b, 0, 0)),
        scratch_shapes=[pltpu.VMEM((rows, 1), F32), pltpu.VMEM((rows, 1), F32), pltpu.VMEM((rows, MLA_KV_LORA), F32),
                        pltpu.VMEM((PAGE_SIZE, wk), BF16)])
    return pl.pallas_call(
        functools.partial(_mla_s_kernel, pg=pg),
        grid_spec=grid_spec, out_shape=jax.ShapeDtypeStruct((bs, rows, MLA_KV_LORA), F32),
        compiler_params=_cparams(2), name='mla_sample',
    )(pt, qcat, kn, *([cckv] * pg), *([ckr] * pg))


def _mla_up_kernel(o_ref, wv_ref, out_ref):
    rows = o_ref.shape[1]
    full = _dot(o_ref[0].astype(BF16), wv_ref[...])
    head = _iota(full.shape, 0) % MLA_HEADS
    keep = (_iota(full.shape, 1) // MLA_V) == head
    out_ref[0] = jnp.where(keep, full, 0.0)


def _mla_up(olat, wv_all):
    bs, rows, _ = olat.shape
    return pl.pallas_call(
        _mla_up_kernel, grid=(bs,),
        in_specs=[pl.BlockSpec((1, rows, MLA_KV_LORA), lambda b: (b, 0, 0)), pl.BlockSpec(wv_all.shape, lambda b: (0, 0))],
        out_specs=pl.BlockSpec((1, rows, MLA_W), lambda b: (b, 0, 0)),
        out_shape=jax.ShapeDtypeStruct((bs, rows, MLA_W), F32), compiler_params=_cparams(1), name='mla_up_sample',
    )(olat, wv_all)


def _dsa_idx_s_kernel(pt_ref, iq_ref, iw_ref, ikn_ref, *rest, pg, topk, n_tok):
    ips = rest[:pg]
    bias_ref, keys_ref = rest[pg:]
    p = pl.program_id(1)
    rows = iq_ref.shape[1]
    npages = keys_ref.shape[0] - 1
    iq = iq_ref[0].astype(BF16)
    iw = iw_ref[0]
    lane = _iota((rows, PAGE_SIZE), 1)
    t_row = _iota((rows, PAGE_SIZE), 0) // IDX_HEADS

    def score_keys(ik):
        sc = _dot_nt(iq, ik.astype(BF16)) * IDX_SCALE
        w = (jnp.maximum(sc, 0.0) * iw).reshape(n_tok, IDX_HEADS, PAGE_SIZE)
        tot = jnp.sum(w, axis=1, keepdims=True)
        return _sort_key(jnp.broadcast_to(tot, w.shape).reshape(rows, PAGE_SIZE))

    @pl.when(p == 0)
    def _():
        keys_ref[npages] = jnp.where(lane <= t_row, score_keys(ikn_ref[0]), INT_MIN)

    for j in range(pg):
        keys_ref[p * pg + j] = score_keys(ips[j][...])

    @pl.when(p == pl.num_programs(1) - 1)
    def _():
        _select_bias(keys_ref, bias_ref.at[0], topk)


def _dsa_idx_sample(pt, iq, iw, ikn, cidx, layer, pg, topk):
    bs, npages = pt.shape
    rows = iq.shape[1]
    fwd = lambda p, j: p * pg + j
    per_b = lambda a: pl.BlockSpec((1,) + a.shape[1:], lambda b, p, pt: (b,) + (0,) * (a.ndim - 1))
    grid_spec = pltpu.PrefetchScalarGridSpec(
        num_scalar_prefetch=1, grid=(bs, npages // pg),
        in_specs=[per_b(iq), per_b(iw), per_b(ikn)] + _page_specs(cidx, IDX_DIM, layer, pg, fwd),
        out_specs=pl.BlockSpec((1, npages + 1, rows, PAGE_SIZE), lambda b, p, pt: (b, 0, 0, 0)),
        scratch_shapes=[pltpu.VMEM((npages + 1, rows, PAGE_SIZE), I32)])
    return pl.pallas_call(
        functools.partial(_dsa_idx_s_kernel, pg=pg, topk=topk, n_tok=rows // IDX_HEADS),
        grid_spec=grid_spec, out_shape=jax.ShapeDtypeStruct((bs, npages + 1, rows, PAGE_SIZE), F32),
        compiler_params=_cparams(2), name='dsa_index_sample',
    )(pt, iq, iw, ikn, *([cidx] * pg))


def _dsa_s_kernel(pt_ref, q2_ref, kn_ref, vn_ref, bn_ref, bp_ref, *rest, pg):
    kps, vps = rest[:pg], rest[pg:2 * pg]
    o_ref, m_s, l_s, acc_s = rest[2 * pg:]
    p = pl.program_id(1)
    rows = q2_ref.shape[1]
    keep, qbd = _qbd(q2_ref[0], rows)

    def page(k, v, bias):
        s = _dot_nt(qbd, k.astype(BF16)) * DSA_SCALE + bias
        m_new, l_new, acc_new = _online((m_s[...], l_s[...], acc_s[...]), s, v.astype(BF16))
        m_s[...], l_s[...], acc_s[...] = m_new, l_new, acc_new

    @pl.when(p == 0)
    def _():
        m_s[...] = jnp.full(m_s.shape, NEG, F32)
        l_s[...] = jnp.zeros(l_s.shape, F32)
        acc_s[...] = jnp.zeros(acc_s.shape, F32)
        page(kn_ref[0], vn_ref[0], bn_ref[0])

    for j in range(pg):
        page(kps[j][...], vps[j][...], bp_ref[j])

    @pl.when(p == pl.num_programs(1) - 1)
    def _():
        o_ref[0] = jnp.where(keep, acc_s[...] / l_s[...], 0.0)


def _dsa_sample(pt, q2, kn, vn, bias, ck, cv, layer, pg):
    bs, npages = pt.shape
    rows = q2.shape[1]
    fwd = lambda p, j: p * pg + j
    per_b = lambda a: pl.BlockSpec((1,) + a.shape[1:], lambda b, p, pt: (b,) + (0,) * (a.ndim - 1))
    grid_spec = pltpu.PrefetchScalarGridSpec(
        num_scalar_prefetch=1, grid=(bs, npages // pg),
        in_specs=[per_b(q2), per_b(kn), per_b(vn),
                  pl.BlockSpec((None, 1, rows, PAGE_SIZE), lambda b, p, pt: (b, npages, 0, 0)),
                  pl.BlockSpec((None, pg, rows, PAGE_SIZE), lambda b, p, pt: (b, p, 0, 0))]
        + _page_specs(ck, LANES, layer, pg, fwd) + _page_specs(cv, LANES, layer, pg, fwd),
        out_specs=pl.BlockSpec((1, rows, LANES), lambda b, p, pt: (b, 0, 0)),
        scratch_shapes=[pltpu.VMEM((rows, 1), F32), pltpu.VMEM((rows, 1), F32), pltpu.VMEM((rows, LANES), F32)])
    return pl.pallas_call(
        functools.partial(_dsa_s_kernel, pg=pg),
        grid_spec=grid_spec, out_shape=jax.ShapeDtypeStruct((bs, rows, LANES), F32),
        compiler_params=_cparams(2), name='dsa_sample',
    )(pt, q2, kn, vn, bias, bias, *([ck] * pg), *([cv] * pg))


def _pad_cols(w):
    parts = []
    for name, real, pad in _GROUPS:
        src = _SRC[name]
        parts.append(w[:, :, src:src + real])
        if pad > real:
            parts.append(jnp.zeros(w.shape[:2] + (pad - real,), w.dtype))
    return jnp.concatenate(parts, axis=-1)


def _rot_tables(pos, half, width, reps):
    inv_freq = jnp.exp(-math.log(ROPE_THETA) * jnp.arange(half, dtype=F32) / half)
    ang = pos.astype(F32)[:, None] * inv_freq[None, :]
    cos, sin = jnp.cos(ang), jnp.sin(ang)
    n = pos.shape[0]
    cos_h = jnp.concatenate([cos, cos, jnp.ones((n, width - 2 * half), F32)], axis=-1)
    sin_h = jnp.concatenate([-sin, sin, jnp.zeros((n, width - 2 * half), F32)], axis=-1)
    return jnp.tile(cos_h, (1, reps)), jnp.tile(sin_h, (1, reps))


def _new_page(a):
    return jnp.pad(a, ((0, 0), (0, PAGE_SIZE - a.shape[1]), (0, 0)))


def kernel(x_prompt, x_sample, cache_fox_k, cache_fox_v, cache_fox_logf, cache_mla_ckv, cache_mla_krope,
           cache_dsa_k, cache_dsa_v, cache_dsa_kidx, cache_mem_k, cache_mem_v, page_table, mem_prompt,
           w_in, b_forget, mla_q_norm, mla_kv_norm, w_mla_q_up, w_mla_kv_up, w_mem_kv, w_branch, w_out,
           ln_gain, ln_bias):
    b, t, d = x_prompt.shape
    bs, ts, _ = x_sample.shape
    depth = w_in.shape[0]
    mem_len = mem_prompt.shape[1]
    npages = page_table.shape[1]
    past_len = npages * PAGE_SIZE
    pool = cache_fox_k.shape[0]
    rows = ts * FOX_HEADS
    alpha = (2 * depth) ** 0.25
    assert FOX_HEADS == MLA_HEADS == DSA_HEADS == IDX_HEADS and ts <= PAGE_SIZE

    tq = min(256, t)
    tr = min(256, t)
    pg = math.gcd(npages, 8)
    n_s = bs * ts
    topk_s = min(TOPK_MAX, (past_len + ts) // 4)

    w_pad = _pad_cols(w_in).astype(BF16)
    bf_pad = jnp.pad(b_forget, ((0, 0), (0, LANES - FOX_HEADS)))[:, None, :]
    qn_pad = jnp.pad(mla_q_norm, ((0, 0), (0, 512 - MLA_Q_LORA)))[:, None, :]
    kvn = mla_kv_norm[:, None, :]
    wq = w_mla_q_up.reshape(depth, MLA_Q_LORA, MLA_HEADS, MLA_NOPE + MLA_ROPE)
    wq_pad = lambda a: jnp.pad(a.reshape(depth, MLA_Q_LORA, -1), ((0, 0), (0, 512 - MLA_Q_LORA), (0, 0))).astype(BF16)
    wq_nope, wq_rope = wq_pad(wq[..., :MLA_NOPE]), wq_pad(wq[..., MLA_NOPE:])
    wkt = jnp.transpose(w_mla_kv_up[..., :MLA_NOPE], (0, 2, 3, 1)).reshape(depth, MLA_HEADS * MLA_NOPE, MLA_KV_LORA).astype(BF16)
    wv = jnp.transpose(w_mla_kv_up[..., MLA_NOPE:], (0, 2, 1, 3))
    wv_slot = jnp.stack([jnp.pad(wv[:, h], ((0, 0), (0, 0), ((h % 2) * MLA_V, (1 - h % 2) * MLA_V)))
                         for h in range(MLA_HEADS)], axis=1).astype(BF16)
    wv_all = jnp.transpose(wv, (0, 2, 1, 3)).reshape(depth, MLA_KV_LORA, MLA_W).astype(BF16)
    w_mem = w_mem_kv.astype(BF16)
    wb, wo = w_branch.astype(BF16), w_out.astype(BF16)
    ln_g, ln_b = ln_gain[:, None, :], ln_bias[:, None, :]

    pos_p = jnp.arange(t, dtype=jnp.int32)
    pos_s = jnp.tile(past_len + jnp.arange(ts, dtype=jnp.int32), bs)
    tabs_p = _rot_tables(pos_p, ROT_DIM // 2, HEAD_DIM, 2) + _rot_tables(pos_p, MLA_ROPE // 2, MLA_ROPE, 4)
    tabs_s = _rot_tables(pos_s, ROT_DIM // 2, HEAD_DIM, 2) + _rot_tables(pos_s, MLA_ROPE // 2, MLA_ROPE, 4)

    c_fk = cache_fox_k.reshape(pool, depth, PAGE_SIZE, LANES)
    c_fv = cache_fox_v.reshape(pool, depth, PAGE_SIZE, LANES)
    c_dk = cache_dsa_k.reshape(pool, depth, PAGE_SIZE, LANES)
    c_dv = cache_dsa_v.reshape(pool, depth, PAGE_SIZE, LANES)
    c_mk = cache_mem_k.reshape(bs, depth, mem_len, MEM_W)
    c_mv = cache_mem_v.reshape(bs, depth, mem_len, MEM_W)

    xp = x_prompt.reshape(b * t, d)
    xs = x_sample.reshape(n_s, d)
    mem2 = mem_prompt.reshape(b * mem_len, d)
    P = {k: [] for k in ('fox_k', 'fox_v', 'logf', 'ckv', 'kr', 'dsa_k', 'dsa_v', 'ik', 'mem_k', 'mem_v')}
    S = {k: [] for k in ('fox_k', 'fox_v', 'logf', 'ckv', 'kr', 'dsa_k', 'dsa_v', 'ik')}

    def col(h2, name):
        off, real, _ = _LAYOUT[name]
        return h2[:, off:off + real]

    for l in range(depth):
        params = (bf_pad[l], qn_pad[l], kvn[l], wq_nope[l], wq_rope[l], wkt[l])

        h2 = _matmul(xp, w_pad[l], min(1024, b * t), 256)
        pr = _prep(h2, tabs_p, params, tr)
        h3 = h2.reshape(b, t, EP)
        r3 = lambda a: a.reshape(b, t, a.shape[-1])
        c, ct = _cumsum(r3(pr['logf']))
        fox_o = _fox_prompt(h3, c, ct, tq)
        mla_o = _mla_prompt(r3(pr['qlat']), r3(pr['qrope']), r3(pr['ckv']), r3(pr['kr4']), wv_slot[l], tq)
        dsa_o = _dsa_prompt(r3(pr['iq']), r3(pr['iw']), r3(pr['ik2']), r3(pr['dq']), r3(pr['dk']), h3, tq)
        mkv = _matmul(mem2, w_mem[l], min(1024, b * mem_len), 2 * MEM_W).reshape(b, mem_len, 2 * MEM_W)
        mem_o = _mem_attention(
            h3, pl.BlockSpec((1, tq, MEM_W), lambda bi, i: (bi, i, _blk('mem_q'))),
            mkv, pl.BlockSpec((1, mem_len, MEM_W), lambda bi, i: (bi, 0, 0)),
            mkv, pl.BlockSpec((1, mem_len, MEM_W), lambda bi, i: (bi, 0, 1)),
            (b, t // tq), (b, t, MEM_W), pl.BlockSpec((1, tq, MEM_W), lambda bi, i: (bi, i, 0)))
        xp_new = _finish(xp, h2, fox_o.reshape(b * t, -1), mla_o.reshape(b * t, -1), dsa_o.reshape(b * t, -1),
                         mem_o.reshape(b * t, -1), wb[l], wo[l], ln_g[l], ln_b[l], tr, alpha)
        P['fox_k'].append(col(h2, 'fox_k')); P['fox_v'].append(col(h2, 'fox_v')); P['logf'].append(pr['logf'])
        P['ckv'].append(pr['ckv']); P['kr'].append(pr['kr4'][:, :MLA_ROPE]); P['dsa_k'].append(pr['dk'])
        P['dsa_v'].append(col(h2, 'dsa_v')); P['ik'].append(pr['ik2'][:, :IDX_DIM])
        P['mem_k'].append(mkv[..., :MEM_W]); P['mem_v'].append(mkv[..., MEM_W:])
        xp = xp_new

        g2 = _matmul(xs, w_pad[l], n_s, 256)
        sr = _prep(g2, tabs_s, params, n_s)
        rows_of = lambda a, w: a.reshape(bs, rows, w)
        new3 = lambda a: a.reshape(bs, ts, a.shape[-1])
        dup = lambda a: jnp.concatenate([a, a], axis=-1)
        lf_new = new3(sr['logf'])
        lrows = jnp.broadcast_to(jnp.transpose(lf_new, (0, 2, 1))[:, None], (bs, ts, FOX_HEADS, ts)).reshape(bs, rows, ts)
        lrows = jnp.pad(lrows, ((0, 0), (0, 0), (0, LANES - ts)))
        fox_s = _fox_sample(page_table, dup(rows_of(col(g2, 'fox_q'), HEAD_DIM)), lrows,
                            _new_page(new3(col(g2, 'fox_k'))), _new_page(new3(col(g2, 'fox_v'))), _new_page(lf_new),
                            c_fk, c_fv, cache_fox_logf, l, pg)
        fox_s = (fox_s[..., :HEAD_DIM] + fox_s[..., HEAD_DIM:]).reshape(n_s, FOX_W)
        qcat = jnp.concatenate([rows_of(sr['qlat'], MLA_KV_LORA), rows_of(sr['qrope'], MLA_ROPE),
                                jnp.zeros((bs, rows, LANES - MLA_ROPE), F32)], axis=-1)
        kn_cat = _new_page(jnp.concatenate([new3(sr['ckv']), new3(sr['kr4'])], axis=-1))
        olat = _mla_sample(page_table, qcat, kn_cat, cache_mla_ckv, cache_mla_krope, l, pg)
        mla_s = _mla_up(olat, wv_all[l]).reshape(bs, rows, MLA_HEADS, MLA_V).sum(axis=2).reshape(n_s, MLA_W)
        bias = _dsa_idx_sample(page_table, rows_of(sr['iq'], IDX_DIM), sr['iw'].reshape(bs, rows, 1),
                               _new_page(new3(sr['ik2'][:, :IDX_DIM])), cache_dsa_kidx, l, pg, topk_s)
        dsa_s = _dsa_sample(page_table, dup(rows_of(sr['dq'], HEAD_DIM)), _new_page(new3(sr['dk'])),
                            _new_page(new3(col(g2, 'dsa_v'))), bias, c_dk, c_dv, l, pg)
        dsa_s = (dsa_s[..., :HEAD_DIM] + dsa_s[..., HEAD_DIM:]).reshape(n_s, DSA_W)
        mem_s = _mem_attention(
            new3(col(g2, 'mem_q')), pl.BlockSpec((1, ts, MEM_W), lambda bi: (bi, 0, 0)),
            c_mk, pl.BlockSpec((None, None, mem_len, MEM_W), lambda bi, l=l: (bi, l, 0, 0)),
            c_mv, pl.BlockSpec((None, None, mem_len, MEM_W), lambda bi, l=l: (bi, l, 0, 0)),
            (bs,), (bs, ts, MEM_W), pl.BlockSpec((1, ts, MEM_W), lambda bi: (bi, 0, 0))).reshape(n_s, MEM_W)
        xs_new = _finish(xs, g2, fox_s, mla_s, dsa_s, mem_s, wb[l], wo[l], ln_g[l], ln_b[l], n_s, alpha)
        S['fox_k'].append(col(g2, 'fox_k')); S['fox_v'].append(col(g2, 'fox_v')); S['logf'].append(sr['logf'])
        S['ckv'].append(sr['ckv']); S['kr'].append(sr['kr4'][:, :MLA_ROPE]); S['dsa_k'].append(sr['dk'])
        S['dsa_v'].append(col(g2, 'dsa_v')); S['ik'].append(sr['ik2'][:, :IDX_DIM])
        xs = xs_new

    def stack(lst, lead, tail):
        return jnp.stack([a.reshape(lead + tail) for a in lst], axis=1)

    kv2 = (FOX_KV_HEADS, HEAD_DIM)
    pl_, sl_ = (b, t), (bs, ts)
    return (xp.reshape(b, t, d), xs.reshape(bs, ts, d),
            stack(P['fox_k'], pl_, kv2), stack(P['fox_v'], pl_, kv2), stack(P['logf'], pl_, (FOX_HEADS,)),
            stack(P['ckv'], pl_, (MLA_KV_LORA,)), stack(P['kr'], pl_, (MLA_ROPE,)),
            stack(P['dsa_k'], pl_, kv2), stack(P['dsa_v'], pl_, kv2), stack(P['ik'], pl_, (IDX_DIM,)),
            stack(P['mem_k'], (b, mem_len), (MEM_HEADS, HEAD_DIM)), stack(P['mem_v'], (b, mem_len), (MEM_HEADS, HEAD_DIM)),
            stack(S['fox_k'], sl_, kv2), stack(S['fox_v'], sl_, kv2), stack(S['logf'], sl_, (FOX_HEADS,)),
            stack(S['ckv'], sl_, (MLA_KV_LORA,)), stack(S['kr'], sl_, (MLA_ROPE,)),
            stack(S['dsa_k'], sl_, kv2), stack(S['dsa_v'], sl_, kv2), stack(S['ik'], sl_, (IDX_DIM,)))
```

```python
import functools
import math

import numpy as np
import jax
import jax.numpy as jnp
from jax import lax
from jax.experimental import pallas as pl
from jax.experimental.pallas import tpu as pltpu

F32 = jnp.float32
BF16 = jnp.bfloat16
I32 = jnp.int32

HEAD_DIM = 64
ROT_DIM = HEAD_DIM // 4
ROPE_THETA = 500000.0
FOX_HEADS = 8
FOX_KV_HEADS = 2
MLA_HEADS = 8
MLA_NOPE = 64
MLA_ROPE = 32
MLA_V = 64
MLA_Q_LORA = 384
MLA_KV_LORA = 256
DSA_HEADS = 8
DSA_KV_HEADS = 2
IDX_HEADS = 8
IDX_DIM = 64
TOPK_MAX = 256
MEM_HEADS = 4
N_BRANCH = 4
PAGE_SIZE = 128
N_HEADS = 8
GROUP = N_HEADS // FOX_KV_HEADS
FOX_W = FOX_HEADS * HEAD_DIM
MLA_W = MLA_HEADS * MLA_V
DSA_W = DSA_HEADS * HEAD_DIM
MEM_W = MEM_HEADS * HEAD_DIM
BRANCH_WIDTHS = (FOX_W, MLA_W, DSA_W, MEM_W)
FOX_SCALE = HEAD_DIM ** -0.5
MLA_SCALE = (MLA_NOPE + MLA_ROPE) ** -0.5
DSA_SCALE = HEAD_DIM ** -0.5
MEM_SCALE = HEAD_DIM ** -0.5
IDX_SCALE = IDX_DIM ** -0.5
IDX_W_SCALE = IDX_HEADS ** -0.5
LN_EPS = 1e-5
RMS_EPS = 1e-6

LANES = 128
VMEM_LIMIT_BYTES = 48 * 2 ** 20

NEG = -1e30
INT_MIN = -2 ** 31

_IN_NAMES = ('fox_q', 'fox_k', 'fox_v', 'fox_f', 'fox_z', 'mla_qa', 'mla_kva', 'mla_kr', 'mla_z',
             'dsa_q', 'dsa_k', 'dsa_v', 'idx_q', 'idx_k', 'idx_w', 'dsa_z', 'mem_q', 'mem_z', 'gates')
_IN_SPLITS = (512, 128, 128, 8, 512, 384, 256, 32, 512, 512, 128, 128, 512, 64, 8, 512, 256, 256, 4096)
_SRC = dict(zip(_IN_NAMES, np.cumsum((0,) + _IN_SPLITS[:-1]).tolist()))
_WID = dict(zip(_IN_NAMES, _IN_SPLITS))
_GROUPS = (
    ('gates', 4096, None),
    ('fox_q', 1024, 'kv'), ('dsa_q', 1024, 'kv'), ('idx_q', 1024, 'lo'),
    ('fox_z', 512, None), ('mla_z', 512, None), ('dsa_z', 512, None), ('mla_qa', 512, None),
    ('mla_kva', 256, None), ('mem_q', 256, None), ('mem_z', 256, None),
    ('fox_k', 128, None), ('fox_v', 128, None), ('dsa_k', 128, None), ('dsa_v', 128, None),
    ('idx_k', 128, None), ('mla_kr', 128, None), ('fox_f', 128, None), ('idx_w', 128, None),
)


def _layout():
    off, out = 0, {}
    for name, pad, _ in _GROUPS:
        assert off % pad == 0
        out[name] = (off, pad)
        off += pad
    return out, off


_LAYOUT, EP = _layout()


def _blk(name):
    off, pad = _LAYOUT[name]
    return off // pad


def _cparams(n_grid):
    return pltpu.CompilerParams(dimension_semantics=('arbitrary',) * n_grid, vmem_limit_bytes=VMEM_LIMIT_BYTES)


def _dot(a, b):
    return jnp.dot(a, b, preferred_element_type=F32)


def _dot_nt(a, b):
    return lax.dot_general(a, b, (((1,), (1,)), ((), ())), preferred_element_type=F32)


def _dot_tn(a, b):
    return lax.dot_general(a, b, (((0,), (0,)), ((), ())), preferred_element_type=F32)


def _split3(x):
    hi = x.astype(BF16)
    r = x - hi.astype(F32)
    mid = r.astype(BF16)
    lo = (r - mid.astype(F32)).astype(BF16)
    return hi, mid, lo


def _iota(shape, dim):
    return lax.broadcasted_iota(I32, shape, dim)


def _rows(blocks):
    return jnp.concatenate(blocks, axis=0)


def _mm_kernel(x_ref, w_ref, o_ref):
    o_ref[...] = _dot(x_ref[...].astype(BF16), w_ref[...].astype(BF16))


def _matmul(x, w, tm, tn):
    m, k = x.shape
    n = w.shape[1]
    assert m % tm == 0 and n % tn == 0
    return pl.pallas_call(
        _mm_kernel,
        grid=(m // tm, n // tn),
        in_specs=[pl.BlockSpec((tm, k), lambda i, j: (i, 0)), pl.BlockSpec((k, tn), lambda i, j: (0, j))],
        out_specs=pl.BlockSpec((tm, tn), lambda i, j: (i, j)),
        out_shape=jax.ShapeDtypeStruct((m, n), F32),
        compiler_params=_cparams(2),
        name='proj_matmul',
    )(x, w)


def _rope(x, cos, sin, half, width):
    w = x.shape[-1]
    d = _iota(x.shape, 1) % width
    up = pltpu.roll(x, w - half, 1)
    dn = pltpu.roll(x, half, 1)
    partner = jnp.where(d < half, up, dn)
    return jnp.where(d < 2 * half, x * cos + partner * sin, x)


def _tile_lanes(x, n):
    return jnp.concatenate([x] * n, axis=-1) if n > 1 else x


def _prep_kernel(qa_ref, kva_ref, dq_ref, dk_ref, iq_ref, ik_ref, kr_ref, ff_ref, iw_ref,
                 ca_ref, sa_ref, cb_ref, sb_ref, bf_ref, qn_ref, kvn_ref, wqn_ref, wqr_ref, wkt_ref,
                 logf_ref, ckv_ref, kro_ref, dqo_ref, dko_ref, iqo_ref, iko_ref, iwo_ref, qlat_ref, qrope_ref):
    ca, sa, cb, sb = ca_ref[...], sa_ref[...], cb_ref[...], sb_ref[...]
    half_a, half_b = ROT_DIM // 2, MLA_ROPE // 2
    logf_ref[...] = jax.nn.log_sigmoid(ff_ref[...] + bf_ref[...])[:, :FOX_HEADS]
    kva = kva_ref[...]
    ckv_ref[...] = kva * lax.rsqrt(jnp.mean(kva * kva, axis=-1, keepdims=True) + RMS_EPS) * kvn_ref[...]
    kro_ref[...] = _rope(kr_ref[...], cb, sb, half_b, MLA_ROPE)
    dqo_ref[...] = _rope(dq_ref[...], _tile_lanes(ca, N_HEADS), _tile_lanes(sa, N_HEADS), half_a, HEAD_DIM) * DSA_SCALE
    dko_ref[...] = _rope(dk_ref[...], ca, sa, half_a, HEAD_DIM)
    iqo_ref[...] = _rope(iq_ref[...], _tile_lanes(ca, N_HEADS), _tile_lanes(sa, N_HEADS), half_a, IDX_DIM) * IDX_SCALE
    iko_ref[...] = _rope(ik_ref[...], ca, sa, half_a, IDX_DIM)
    iwo_ref[...] = (iw_ref[...] * IDX_W_SCALE)[:, :IDX_HEADS]
    qa = qa_ref[...]
    ms = jnp.sum(qa * qa, axis=-1, keepdims=True) * (1.0 / MLA_Q_LORA)
    qan = (qa * lax.rsqrt(ms + RMS_EPS) * qn_ref[...]).astype(BF16)
    q_nope = _dot(qan, wqn_ref[...]).astype(BF16)
    q_rope = _dot(qan, wqr_ref[...])
    qrope_ref[...] = _rope(q_rope, _tile_lanes(cb, N_HEADS), _tile_lanes(sb, N_HEADS), half_b, MLA_ROPE) * MLA_SCALE
    for h in range(MLA_HEADS):
        qlat_ref[:, MLA_KV_LORA * h:MLA_KV_LORA * (h + 1)] = _dot(
            q_nope[:, LANES * h:LANES * (h + 1)], wkt_ref[LANES * h:LANES * (h + 1), :]) * MLA_SCALE


def _prep(h2, tabs, params, tr):
    n = h2.shape[0]
    nt = tabs[0].shape[0] // tr
    row = lambda name: pl.BlockSpec((tr, _LAYOUT[name][1]), lambda i, name=name: (i, _blk(name)))
    tab = pl.BlockSpec((tr, LANES), lambda i: (i % nt, 0))
    full = lambda a: pl.BlockSpec(a.shape, lambda i: (0,) * a.ndim)
    names = ('mla_qa', 'mla_kva', 'dsa_q', 'dsa_k', 'idx_q', 'idx_k', 'mla_kr', 'fox_f', 'idx_w')
    in_specs = [row(nm) for nm in names] + [tab] * 4 + [full(p) for p in params]
    outs = [('logf', FOX_HEADS), ('ckv', MLA_KV_LORA), ('kr', LANES), ('dq', N_HEADS * LANES), ('dk', LANES),
            ('iq', N_HEADS * LANES), ('ik', LANES), ('iw', IDX_HEADS), ('qlat', MLA_HEADS * MLA_KV_LORA),
            ('qrope', N_HEADS * LANES)]
    res = pl.pallas_call(
        _prep_kernel,
        grid=(n // tr,),
        in_specs=in_specs,
        out_specs=[pl.BlockSpec((tr, w), lambda i: (i, 0)) for _, w in outs],
        out_shape=[jax.ShapeDtypeStruct((n, w), F32) for _, w in outs],
        compiler_params=_cparams(1),
        name='prep_rows',
    )(*([h2] * len(names)), *tabs, *params)
    return dict(zip([k for k, _ in outs], res))


def _cumsum_kernel(lf_ref, c_ref, ct_ref, *, blk):
    t = lf_ref.shape[1]
    r, c = _iota((blk, blk), 0), _iota((blk, blk), 1)
    tri = jnp.where(c <= r, 1.0, 0.0).astype(BF16)
    trit = jnp.where(r <= c, 1.0, 0.0).astype(BF16)
    carry = jnp.zeros((1, FOX_HEADS), F32)
    carry_t = jnp.zeros((FOX_HEADS, 1), F32)
    for i in range(t // blk):
        parts = _split3(lf_ref[0, blk * i:blk * (i + 1), :])
        cb = sum(_dot(tri, p) for p in parts) + carry
        cbt = sum(_dot_tn(p, trit) for p in parts) + carry_t
        c_ref[0, blk * i:blk * (i + 1), :] = cb
        ct_ref[0, :, blk * i:blk * (i + 1)] = cbt
        carry = cb[blk - 1:blk, :]
        carry_t = cbt[:, blk - 1:blk]


def _cumsum(lf):
    b, t, h = lf.shape
    return pl.pallas_call(
        functools.partial(_cumsum_kernel, blk=min(256, t)),
        grid=(b,),
        in_specs=[pl.BlockSpec((1, t, h), lambda i: (i, 0, 0))],
        out_specs=[pl.BlockSpec((1, t, h), lambda i: (i, 0, 0)), pl.BlockSpec((1, h, t), lambda i: (i, 0, 0))],
        out_shape=[jax.ShapeDtypeStruct((b, t, h), F32), jax.ShapeDtypeStruct((b, h, t), F32)],
        compiler_params=_cparams(1),
        name='forget_cumsum',
    )(lf)


def _online(state, s, pv):
    m_i, l_i, acc = state
    m_new = jnp.maximum(m_i, jnp.max(s, axis=-1, keepdims=True))
    alpha = jnp.exp(m_i - m_new)
    p = jnp.exp(s - m_new)
    l_new = alpha * l_i + jnp.sum(p, axis=-1, keepdims=True)
    acc_new = alpha * acc + pv(p.astype(BF16))
    return m_new, l_new, acc_new


def _init_state(rows, dv):
    return (jnp.full((rows, 1), NEG, F32), jnp.zeros((rows, 1), F32), jnp.zeros((rows, dv), F32))


def _causal_rows(s, i, j, tq, tk):
    qpos = i * tq + _iota(s.shape, 0) % tq
    kpos = j * tk + _iota(s.shape, 1)
    return jnp.where(kpos <= qpos, s, NEG)


def _compact_pairs(o, g, tq, o_ref, first_pair):
    lo = _iota((tq, LANES), 1) < HEAD_DIM
    for m in range(GROUP // 2):
        a, b = o[2 * m * tq:(2 * m + 1) * tq], o[(2 * m + 1) * tq:(2 * m + 2) * tq]
        pair = jnp.where(lo, a, pltpu.roll(b, HEAD_DIM, 1)) if g == 0 else jnp.where(lo, pltpu.roll(a, HEAD_DIM, 1), b)
        o_ref[0, :, LANES * (first_pair + m):LANES * (first_pair + m + 1)] = pair


def _fox_p_kernel(q_ref, k_ref, v_ref, c_ref, ct_ref, o_ref, *, tq, tk, nc):
    i = pl.program_id(1)
    nfull = (i * tq) // tk
    for g in range(FOX_KV_HEADS):
        heads = range(GROUP * g, GROUP * (g + 1))
        qg = _rows([(q_ref[0, :, LANES * h:LANES * (h + 1)] * FOX_SCALE).astype(BF16) for h in heads])
        cq = _rows([c_ref[0, :, h:h + 1] for h in heads])

        def chunk(j, st, masked, qg=qg, cq=cq, heads=heads):
            kc = k_ref[0, pl.ds(j * tk, tk), :].astype(BF16)
            vc = v_ref[0, pl.ds(j * tk, tk), :].astype(BF16)
            ck = _rows([jnp.broadcast_to(ct_ref[0, pl.ds(h * nc + j, 1), :], (tq, tk)) for h in heads])
            s = _dot_nt(qg, kc) + (cq - ck)
            if masked:
                s = _causal_rows(s, i, j, tq, tk)
            return _online(st, s, lambda p: _dot(p, vc))

        st = lax.fori_loop(0, nfull, functools.partial(chunk, masked=False), _init_state(GROUP * tq, LANES))
        _, l_i, acc = chunk(nfull, st, True)
        _compact_pairs(acc / l_i, g, tq, o_ref, g * (GROUP // 2))


def _fox_prompt(h3, c, ct, tq, tk):
    b, t, _ = h3.shape
    nc = t // tk
    ctr = ct.reshape(b, FOX_HEADS * nc, tk)
    return pl.pallas_call(
        functools.partial(_fox_p_kernel, tq=tq, tk=tk, nc=nc),
        grid=(b, t // tq),
        in_specs=[pl.BlockSpec((1, tq, N_HEADS * LANES), lambda bi, i: (bi, i, _blk('fox_q'))),
                  pl.BlockSpec((1, t, LANES), lambda bi, i: (bi, 0, _blk('fox_k'))),
                  pl.BlockSpec((1, t, LANES), lambda bi, i: (bi, 0, _blk('fox_v'))),
                  pl.BlockSpec((1, tq, FOX_HEADS), lambda bi, i: (bi, i, 0)),
                  pl.BlockSpec((1, FOX_HEADS * nc, tk), lambda bi, i: (bi, 0, 0))],
        out_specs=pl.BlockSpec((1, tq, FOX_W), lambda bi, i: (bi, i, 0)),
        out_shape=jax.ShapeDtypeStruct((b, t, FOX_W), F32),
        compiler_params=_cparams(2),
        name='fox_prompt',
    )(h3, h3, h3, c, ctr)


def _mla_p_kernel(ql_ref, qr_ref, ckv_ref, kr_ref, wv_ref, o_ref, *, tq, tk):
    i = pl.program_id(1)
    nfull = (i * tq) // tk
    ql = _rows([ql_ref[0, :, MLA_KV_LORA * h:MLA_KV_LORA * (h + 1)].astype(BF16) for h in range(MLA_HEADS)])
    qr = _rows([qr_ref[0, :, LANES * h:LANES * (h + 1)].astype(BF16) for h in range(MLA_HEADS)])

    def chunk(j, st, masked):
        kc = ckv_ref[0, pl.ds(j * tk, tk), :].astype(BF16)
        kr = kr_ref[0, pl.ds(j * tk, tk), :].astype(BF16)
        s = _dot_nt(ql, kc) + _dot_nt(qr, kr)
        if masked:
            s = _causal_rows(s, i, j, tq, tk)
        return _online(st, s, lambda p: _dot(p, kc))

    st = lax.fori_loop(0, nfull, functools.partial(chunk, masked=False), _init_state(MLA_HEADS * tq, MLA_KV_LORA))
    _, l_i, acc = chunk(nfull, st, True)
    o = (acc / l_i).astype(BF16)
    for m in range(MLA_HEADS // 2):
        o_ref[0, :, LANES * m:LANES * (m + 1)] = (_dot(o[2 * m * tq:(2 * m + 1) * tq], wv_ref[2 * m])
                                                  + _dot(o[(2 * m + 1) * tq:(2 * m + 2) * tq], wv_ref[2 * m + 1]))


def _mla_prompt(qlat, qrope, ckv, kr, wv_slot, tq, tk):
    b, t, _ = qlat.shape
    return pl.pallas_call(
        functools.partial(_mla_p_kernel, tq=tq, tk=tk),
        grid=(b, t // tq),
        in_specs=[pl.BlockSpec((1, tq, qlat.shape[2]), lambda bi, i: (bi, i, 0)),
                  pl.BlockSpec((1, tq, qrope.shape[2]), lambda bi, i: (bi, i, 0)),
                  pl.BlockSpec((1, t, MLA_KV_LORA), lambda bi, i: (bi, 0, 0)),
                  pl.BlockSpec((1, t, LANES), lambda bi, i: (bi, 0, 0)),
                  pl.BlockSpec(wv_slot.shape, lambda bi, i: (0, 0, 0))],
        out_specs=pl.BlockSpec((1, tq, MLA_W), lambda bi, i: (bi, i, 0)),
        out_shape=jax.ShapeDtypeStruct((b, t, MLA_W), F32),
        compiler_params=_cparams(2),
        name='mla_prompt',
    )(qlat, qrope, ckv, kr, wv_slot)


def _sort_key(score):
    b = lax.bitcast_convert_type(jnp.where(score == 0.0, 0.0, score), I32)
    return b ^ ((b >> 31) & 0x7FFFFFFF)


def _select_bias(keys_ref, bias_ref, topk, n_used):
    nc, rows, tk = keys_ref.shape
    kf = float(topk)

    def count(pred):
        def body(j, acc):
            w = jnp.where(pred(keys_ref[j], j), 1.0, 0.0)
            return acc + sum(w[:, LANES * c:LANES * (c + 1)] for c in range(tk // LANES))
        return jnp.sum(lax.fori_loop(0, n_used, body, jnp.zeros((rows, LANES), F32)), axis=-1, keepdims=True)

    cur = jnp.where(count(lambda x, j: x >= 0) >= kf, 0, INT_MIN).astype(I32)

    def vbody(b, cur):
        cand = cur | jnp.left_shift(jnp.int32(1), 30 - b)
        return jnp.where(count(lambda x, j: x >= cand) >= kf, cand, cur)

    thr = lax.fori_loop(0, 31, vbody, cur)
    need = kf - count(lambda x, j: x > thr)
    lane = _iota((rows, tk), 1)
    nbits = max(1, int(math.ceil(math.log2(nc * tk))))

    def tie_break():
        def ibody(b, p):
            cand = p + jnp.left_shift(jnp.int32(1), nbits - 1 - b)
            return jnp.where(count(lambda x, j: (x == thr) & (j * tk + lane < cand)) < need, cand, p)
        return lax.fori_loop(0, nbits, ibody, jnp.zeros((rows, 1), I32))

    excess = jnp.max(jnp.where(thr > INT_MIN, count(lambda x, j: x == thr) - need, 0.0))
    p = lax.cond(excess > 0.0, tie_break, lambda: jnp.full((rows, 1), nc * tk, I32))

    def wbody(j, carry):
        x = keys_ref[j]
        sel = (x > INT_MIN) & ((x > thr) | ((x == thr) & (j * tk + lane <= p)))
        bias_ref[j] = jnp.where(sel, 0.0, NEG)
        return carry

    lax.fori_loop(0, n_used, wbody, 0)


def _dsa_p_kernel(iq_ref, iw_ref, ik_ref, q_ref, k_ref, v_ref, o_ref, keys_ref, bias_ref, *, tq, tk, topk):
    i = pl.program_id(1)
    n_used = (i * tq) // tk + 1
    iq = _rows([iq_ref[0, :, LANES * h:LANES * (h + 1)].astype(BF16) for h in range(IDX_HEADS)])
    iw = _rows([iw_ref[0, :, h:h + 1] for h in range(IDX_HEADS)])

    def ibody(j, carry):
        w = jnp.maximum(_dot_nt(iq, ik_ref[0, pl.ds(j * tk, tk), :].astype(BF16)), 0.0) * iw
        score = w[0:tq]
        for h in range(1, IDX_HEADS):
            score = score + w[h * tq:(h + 1) * tq]
        qpos = i * tq + _iota((tq, tk), 0)
        kpos = j * tk + _iota((tq, tk), 1)
        keys_ref[j] = jnp.where(kpos <= qpos, _sort_key(score), INT_MIN)
        return carry

    lax.fori_loop(0, n_used, ibody, 0)
    _select_bias(keys_ref, bias_ref, topk, n_used)

    for g in range(DSA_KV_HEADS):
        qg = _rows([q_ref[0, :, LANES * h:LANES * (h + 1)].astype(BF16) for h in range(GROUP * g, GROUP * (g + 1))])

        def chunk(j, st, qg=qg):
            kc = k_ref[0, pl.ds(j * tk, tk), :].astype(BF16)
            vc = v_ref[0, pl.ds(j * tk, tk), :].astype(BF16)
            bias = bias_ref[j]
            return _online(st, _dot_nt(qg, kc) + _rows([bias] * GROUP), lambda p: _dot(p, vc))

        _, l_i, acc = lax.fori_loop(0, n_used, chunk, _init_state(GROUP * tq, LANES))
        _compact_pairs(acc / l_i, g, tq, o_ref, g * (GROUP // 2))


def _dsa_prompt(iq, iw, ik, dq, dk, h3, tq, tk):
    b, t, _ = dq.shape
    nc = t // tk
    topk = min(TOPK_MAX, t // 4)
    return pl.pallas_call(
        functools.partial(_dsa_p_kernel, tq=tq, tk=tk, topk=topk),
        grid=(b, t // tq),
        in_specs=[pl.BlockSpec((1, tq, N_HEADS * LANES), lambda bi, i: (bi, i, 0)),
                  pl.BlockSpec((1, tq, IDX_HEADS), lambda bi, i: (bi, i, 0)),
                  pl.BlockSpec((1, t, LANES), lambda bi, i: (bi, 0, 0)),
                  pl.BlockSpec((1, tq, N_HEADS * LANES), lambda bi, i: (bi, i, 0)),
                  pl.BlockSpec((1, t, LANES), lambda bi, i: (bi, 0, 0)),
                  pl.BlockSpec((1, t, LANES), lambda bi, i: (bi, 0, _blk('dsa_v')))],
        out_specs=pl.BlockSpec((1, tq, DSA_W), lambda bi, i: (bi, i, 0)),
        out_shape=jax.ShapeDtypeStruct((b, t, DSA_W), F32),
        scratch_shapes=[pltpu.VMEM((nc, tq, tk), I32), pltpu.VMEM((nc, tq, tk), F32)],
        compiler_params=_cparams(2),
        name='dsa_prompt',
    )(iq, iw, ik, dq, dk, h3)


def _mem_kernel(q_ref, k_ref, v_ref, o_ref):
    tq = q_ref.shape[1]
    half = _iota((k_ref.shape[-2], LANES), 1) // HEAD_DIM
    for m in range(MEM_HEADS // 2):
        qp = q_ref[0, :, LANES * m:LANES * (m + 1)].astype(BF16)
        kp = k_ref[:, LANES * m:LANES * (m + 1)] if k_ref.ndim == 2 else k_ref[0, :, LANES * m:LANES * (m + 1)]
        vp = v_ref[:, LANES * m:LANES * (m + 1)] if v_ref.ndim == 2 else v_ref[0, :, LANES * m:LANES * (m + 1)]
        pair = jnp.zeros((tq, LANES), F32)
        for s in range(2):
            kc = jnp.where(half == s, kp, 0.0).astype(BF16)
            vc = jnp.where(half == s, vp, 0.0).astype(BF16)
            sc = _dot_nt(qp, kc) * MEM_SCALE
            p = jnp.exp(sc - jnp.max(sc, axis=-1, keepdims=True))
            p = p / jnp.sum(p, axis=-1, keepdims=True)
            pair = pair + _dot(p.astype(BF16), vc)
        o_ref[0, :, LANES * m:LANES * (m + 1)] = pair


def _mem_attention(q_arr, q_spec, k_arr, k_spec, v_arr, v_spec, grid, out_shape, out_spec):
    return pl.pallas_call(
        _mem_kernel, grid=grid, in_specs=[q_spec, k_spec, v_spec], out_specs=out_spec,
        out_shape=jax.ShapeDtypeStruct(out_shape, F32), compiler_params=_cparams(len(grid)), name='mem_attention',
    )(q_arr, k_arr, v_arr)


def _finish_kernel(x_ref, fo_ref, mo_ref, do_ref, eo_ref, fz_ref, mz_ref, dz_ref, ez_ref, gt_ref,
                   wb_ref, wo_ref, g_ref, b_ref, o_ref, *, alpha):
    offs = np.cumsum((0,) + BRANCH_WIDTHS).tolist()
    d = x_ref.shape[-1]
    y = jnp.zeros(x_ref.shape, F32)
    for n, (o_r, z_r) in enumerate(((fo_ref, fz_ref), (mo_ref, mz_ref), (do_ref, dz_ref), (eo_ref, ez_ref))):
        z = z_r[...]
        u = o_r[...] * (z * jax.nn.sigmoid(z))
        pb = _dot(u.astype(BF16), wb_ref[offs[n]:offs[n + 1], :])
        y = y + jax.nn.sigmoid(gt_ref[:, d * n:d * (n + 1)]) * pb
    r = alpha * x_ref[...] + _dot(y.astype(BF16), wo_ref[...])
    mu = jnp.mean(r, axis=-1, keepdims=True)
    var = jnp.mean(jnp.square(r - mu), axis=-1, keepdims=True)
    o_ref[...] = (r - mu) * lax.rsqrt(var + LN_EPS) * g_ref[...] + b_ref[...]


def _finish(x2, h2, fo, mo, do, eo, wb, wo, g, b, tr, alpha):
    n, d = x2.shape
    row = lambda w: pl.BlockSpec((tr, w), lambda i: (i, 0))
    hcol = lambda name: pl.BlockSpec((tr, _LAYOUT[name][1]), lambda i, name=name: (i, _blk(name)))
    full = lambda a: pl.BlockSpec(a.shape, lambda i: (0,) * a.ndim)
    return pl.pallas_call(
        functools.partial(_finish_kernel, alpha=alpha),
        grid=(n // tr,),
        in_specs=[row(d), row(FOX_W), row(MLA_W), row(DSA_W), row(MEM_W),
                  hcol('fox_z'), hcol('mla_z'), hcol('dsa_z'), hcol('mem_z'), hcol('gates'),
                  full(wb), full(wo), full(g), full(b)],
        out_specs=row(d),
        out_shape=jax.ShapeDtypeStruct((n, d), F32),
        compiler_params=_cparams(1),
        name='finish',
    )(x2, fo, mo, do, eo, h2, h2, h2, h2, h2, wb, wo, g, b)


def _page_specs(shape, layer, pg, page_of):
    return [pl.BlockSpec((None, None) + tuple(shape[2:]),
                         lambda b, p, pt, j=j: (pt[b, page_of(p, j)], layer, 0, 0)) for j in range(pg)]


def _per_b(a):
    return pl.BlockSpec((1,) + a.shape[1:], lambda b, p, pt: (b,) + (0,) * (a.ndim - 1))


def _scratch_init(m_s, l_s, acc_s):
    m_s[...] = jnp.full(m_s.shape, NEG, F32)
    l_s[...] = jnp.zeros(l_s.shape, F32)
    acc_s[...] = jnp.zeros(acc_s.shape, F32)


def _scratch_update(m_s, l_s, acc_s, s, pv):
    m_new, l_new, acc_new = _online((m_s[...], l_s[...], acc_s[...]), s, pv)
    m_s[...], l_s[...], acc_s[...] = m_new, l_new, acc_new


def _paged_pv(vts):
    def pv(p):
        return sum(_dot_nt(p[:, PAGE_SIZE * j:PAGE_SIZE * (j + 1)], vts[j][...].astype(BF16)) for j in range(len(vts)))
    return pv


def _fox_s_kernel(pt_ref, q_ref, lrows_ref, kn_ref, vn_ref, lfn_ref, *rest, pg, n_tok):
    kps, vps, lps = rest[:pg], rest[pg:2 * pg], rest[2 * pg:3 * pg]
    o_ref, m_s, l_s, acc_s, carry_s = rest[3 * pg:]
    p = pl.program_id(1)
    rows = q_ref.shape[1]
    q = (q_ref[0] * FOX_SCALE).astype(BF16)
    lane = _iota((rows, PAGE_SIZE), 1)
    t_row = _iota((rows, PAGE_SIZE), 0) // N_HEADS
    sp = _iota((PAGE_SIZE, 2 * PAGE_SIZE), 0)
    cc = _iota((PAGE_SIZE, 2 * PAGE_SIZE), 1)
    msk = jnp.where(((cc < PAGE_SIZE) & (sp >= cc)) | ((cc >= PAGE_SIZE) & (sp > cc - PAGE_SIZE)), 1.0, 0.0).astype(BF16)
    rowc = jnp.sum(jnp.where(lane > t_row, lrows_ref[0], 0.0), axis=-1, keepdims=True)

    def logits(kts, lfts, carry):
        out = []
        for kt, lft in zip(kts, lfts):
            sfx = sum(_dot(part, msk) for part in _split3(lft))
            bias8 = sfx[:, PAGE_SIZE:] + carry
            carry = carry + sfx[:, 0:1]
            out.append(_dot(q, kt.astype(BF16)) + (_rows([bias8] * n_tok) - rowc))
        return out, carry

    @pl.when(p == 0)
    def _():
        _scratch_init(m_s, l_s, acc_s)
        (s,), carry = logits([kn_ref[0]], [lfn_ref[0]], jnp.zeros(carry_s.shape, F32))
        carry_s[...] = carry
        _scratch_update(m_s, l_s, acc_s, jnp.where(lane <= t_row, s, NEG), _paged_pv([vn_ref.at[0]]))

    ss, carry = logits([r[...] for r in kps], [r[...] for r in lps], carry_s[...])
    carry_s[...] = carry
    _scratch_update(m_s, l_s, acc_s, jnp.concatenate(ss, axis=1), _paged_pv(vps))

    @pl.when(p == pl.num_programs(1) - 1)
    def _():
        o_ref[0] = acc_s[...] / l_s[...]


def _fox_sample(pt, q, lrows, knt, vnt, lfnt, ckt, cvt, clft, layer, pg):
    bs, npages = pt.shape
    rows = q.shape[1]
    rev = lambda p, j: npages - 1 - (p * pg + j)
    grid_spec = pltpu.PrefetchScalarGridSpec(
        num_scalar_prefetch=1, grid=(bs, npages // pg),
        in_specs=[_per_b(q), _per_b(lrows), _per_b(knt), _per_b(vnt), _per_b(lfnt)]
        + _page_specs(ckt.shape, layer, pg, rev) + _page_specs(cvt.shape, layer, pg, rev)
        + _page_specs(clft.shape, layer, pg, rev),
        out_specs=pl.BlockSpec((1, rows, LANES), lambda b, p, pt: (b, 0, 0)),
        scratch_shapes=[pltpu.VMEM((rows, 1), F32), pltpu.VMEM((rows, 1), F32), pltpu.VMEM((rows, LANES), F32),
                        pltpu.VMEM((FOX_HEADS, 1), F32)])
    return pl.pallas_call(
        functools.partial(_fox_s_kernel, pg=pg, n_tok=rows // N_HEADS),
        grid_spec=grid_spec, out_shape=jax.ShapeDtypeStruct((bs, rows, LANES), F32),
        compiler_params=_cparams(2), name='fox_sample',
    )(pt, q, lrows, knt, vnt, lfnt, *([ckt] * pg), *([cvt] * pg), *([clft] * pg))


def _mla_s_kernel(pt_ref, ql_ref, qr_ref, cn_ref, rn_ref, *rest, pg):
    cps, rps = rest[:pg], rest[pg:2 * pg]
    o_ref, m_s, l_s, acc_s = rest[2 * pg:]
    p = pl.program_id(1)
    rows = ql_ref.shape[1]
    ql = ql_ref[0].astype(BF16)
    qr = qr_ref[0].astype(BF16)
    lane = _iota((rows, PAGE_SIZE), 1)
    t_row = _iota((rows, PAGE_SIZE), 0) // N_HEADS

    def logits(c, rt):
        return _dot_nt(ql, c.astype(BF16)) + _dot(qr, rt.astype(BF16))

    def pv_of(cs):
        def pv(pr):
            return sum(_dot(pr[:, PAGE_SIZE * j:PAGE_SIZE * (j + 1)], cs[j][...].astype(BF16)) for j in range(len(cs)))
        return pv

    @pl.when(p == 0)
    def _():
        _scratch_init(m_s, l_s, acc_s)
        s = logits(cn_ref[0], rn_ref[0])
        _scratch_update(m_s, l_s, acc_s, jnp.where(lane <= t_row, s, NEG), pv_of([cn_ref.at[0]]))

    s = jnp.concatenate([logits(cps[j][...], rps[j][...]) for j in range(pg)], axis=1)
    _scratch_update(m_s, l_s, acc_s, s, pv_of(cps))

    @pl.when(p == pl.num_programs(1) - 1)
    def _():
        o_ref[0] = acc_s[...] / l_s[...]


def _mla_sample(pt, ql, qr, cn, rnt, cckv, ckrt, layer, pg):
    bs, npages = pt.shape
    rows = ql.shape[1]
    fwd = lambda p, j: p * pg + j
    grid_spec = pltpu.PrefetchScalarGridSpec(
        num_scalar_prefetch=1, grid=(bs, npages // pg),
        in_specs=[_per_b(ql), _per_b(qr), _per_b(cn), _per_b(rnt)]
        + _page_specs(cckv.shape, layer, pg, fwd) + _page_specs(ckrt.shape, layer, pg, fwd),
        out_specs=pl.BlockSpec((1, rows, MLA_KV_LORA), lambda b, p, pt: (b, 0, 0)),
        scratch_shapes=[pltpu.VMEM((rows, 1), F32), pltpu.VMEM((rows, 1), F32), pltpu.VMEM((rows, MLA_KV_LORA), F32)])
    return pl.pallas_call(
        functools.partial(_mla_s_kernel, pg=pg),
        grid_spec=grid_spec, out_shape=jax.ShapeDtypeStruct((bs, rows, MLA_KV_LORA), F32),
        compiler_params=_cparams(2), name='mla_sample',
    )(pt, ql, qr, cn, rnt, *([cckv] * pg), *([ckrt] * pg))


def _mla_up_kernel(o_ref, wv_ref, out_ref):
    full = _dot(o_ref[0].astype(BF16), wv_ref[...])
    head = _iota(full.shape, 0) % MLA_HEADS
    keep = (_iota(full.shape, 1) // MLA_V) == head
    out_ref[0] = jnp.where(keep, full, 0.0)


def _mla_up(olat, wv_all):
    bs, rows, _ = olat.shape
    return pl.pallas_call(
        _mla_up_kernel, grid=(bs,),
        in_specs=[pl.BlockSpec((1, rows, MLA_KV_LORA), lambda b: (b, 0, 0)), pl.BlockSpec(wv_all.shape, lambda b: (0, 0))],
        out_specs=pl.BlockSpec((1, rows, MLA_W), lambda b: (b, 0, 0)),
        out_shape=jax.ShapeDtypeStruct((bs, rows, MLA_W), F32), compiler_params=_cparams(1), name='mla_up_sample',
    )(olat, wv_all)


def _dsa_idx_s_kernel(pt_ref, iq_ref, iw_ref, ikn_ref, *rest, pg, topk, n_tok):
    ips = rest[:pg]
    bias_ref, keys_ref = rest[pg:]
    p = pl.program_id(1)
    rows = iq_ref.shape[1]
    steps = keys_ref.shape[0] - 1
    iq = iq_ref[0].astype(BF16)
    iw = iw_ref[0]
    lane = _iota((rows, PAGE_SIZE), 1)
    t_row = _iota((rows, PAGE_SIZE), 0) // IDX_HEADS

    def score_keys(ikt):
        w = (jnp.maximum(_dot(iq, ikt.astype(BF16)), 0.0) * iw).reshape(n_tok, IDX_HEADS, PAGE_SIZE)
        tot = jnp.sum(w, axis=1, keepdims=True)
        return _sort_key(jnp.broadcast_to(tot, w.shape).reshape(rows, PAGE_SIZE))

    @pl.when(p == 0)
    def _():
        keys_ref[steps] = jnp.full(keys_ref.shape[1:], INT_MIN, I32)
        keys_ref[steps, :, 0:PAGE_SIZE] = jnp.where(lane <= t_row, score_keys(ikn_ref[0]), INT_MIN)

    for j in range(pg):
        keys_ref[p, :, PAGE_SIZE * j:PAGE_SIZE * (j + 1)] = score_keys(ips[j][...])

    @pl.when(p == steps - 1)
    def _():
        _select_bias(keys_ref, bias_ref.at[0], topk, steps + 1)


def _dsa_idx_sample(pt, iq, iw, iknt, cidxt, layer, pg, topk):
    bs, npages = pt.shape
    rows = iq.shape[1]
    fwd = lambda p, j: p * pg + j
    steps = npages // pg
    grid_spec = pltpu.PrefetchScalarGridSpec(
        num_scalar_prefetch=1, grid=(bs, steps),
        in_specs=[_per_b(iq), _per_b(iw), _per_b(iknt)] + _page_specs(cidxt.shape, layer, pg, fwd),
        out_specs=pl.BlockSpec((1, steps + 1, rows, pg * PAGE_SIZE), lambda b, p, pt: (b, 0, 0, 0)),
        scratch_shapes=[pltpu.VMEM((steps + 1, rows, pg * PAGE_SIZE), I32)])
    return pl.pallas_call(
        functools.partial(_dsa_idx_s_kernel, pg=pg, topk=topk, n_tok=rows // IDX_HEADS),
        grid_spec=grid_spec, out_shape=jax.ShapeDtypeStruct((bs, steps + 1, rows, pg * PAGE_SIZE), F32),
        compiler_params=_cparams(2), name='dsa_index_sample',
    )(pt, iq, iw, iknt, *([cidxt] * pg))


def _dsa_s_kernel(pt_ref, q_ref, kn_ref, vn_ref, bn_ref, bp_ref, *rest, pg):
    kps, vps = rest[:pg], rest[pg:2 * pg]
    o_ref, m_s, l_s, acc_s = rest[2 * pg:]
    p = pl.program_id(1)
    q = q_ref[0].astype(BF16)

    @pl.when(p == 0)
    def _():
        _scratch_init(m_s, l_s, acc_s)
        _scratch_update(m_s, l_s, acc_s, _dot(q, kn_ref[0].astype(BF16)) + bn_ref[0, :, 0:PAGE_SIZE],
                        _paged_pv([vn_ref.at[0]]))

    s = jnp.concatenate([_dot(q, kps[j][...].astype(BF16)) for j in range(pg)], axis=1) + bp_ref[0]
    _scratch_update(m_s, l_s, acc_s, s, _paged_pv(vps))

    @pl.when(p == pl.num_programs(1) - 1)
    def _():
        o_ref[0] = acc_s[...] / l_s[...]


def _dsa_sample(pt, q, knt, vnt, bias, ckt, cvt, layer, pg):
    bs, npages = pt.shape
    rows = q.shape[1]
    fwd = lambda p, j: p * pg + j
    grid_spec = pltpu.PrefetchScalarGridSpec(
        num_scalar_prefetch=1, grid=(bs, npages // pg),
        in_specs=[_per_b(q), _per_b(knt), _per_b(vnt),
                  pl.BlockSpec((None, 1, rows, pg * PAGE_SIZE), lambda b, p, pt: (b, npages // pg, 0, 0)),
                  pl.BlockSpec((None, 1, rows, pg * PAGE_SIZE), lambda b, p, pt: (b, p, 0, 0))]
        + _page_specs(ckt.shape, layer, pg, fwd) + _page_specs(cvt.shape, layer, pg, fwd),
        out_specs=pl.BlockSpec((1, rows, LANES), lambda b, p, pt: (b, 0, 0)),
        scratch_shapes=[pltpu.VMEM((rows, 1), F32), pltpu.VMEM((rows, 1), F32), pltpu.VMEM((rows, LANES), F32)])
    return pl.pallas_call(
        functools.partial(_dsa_s_kernel, pg=pg),
        grid_spec=grid_spec, out_shape=jax.ShapeDtypeStruct((bs, rows, LANES), F32),
        compiler_params=_cparams(2), name='dsa_sample',
    )(pt, q, knt, vnt, bias, bias, *([ckt] * pg), *([cvt] * pg))


def _pad_cols(w):
    zeros = lambda n: jnp.zeros(w.shape[:2] + (n,), w.dtype)
    parts = []
    for name, pad, slotting in _GROUPS:
        src, real = _SRC[name], _WID[name]
        if slotting is None:
            parts.append(w[:, :, src:src + real])
            if pad > real:
                parts.append(zeros(pad - real))
            continue
        for h in range(N_HEADS):
            head = w[:, :, src + HEAD_DIM * h:src + HEAD_DIM * (h + 1)]
            first = slotting == 'lo' or h // GROUP == 0
            parts.extend([head, zeros(HEAD_DIM)] if first else [zeros(HEAD_DIM), head])
    return jnp.concatenate(parts, axis=-1)


def _rot_tables(pos, half, width, reps):
    inv_freq = jnp.exp(-math.log(ROPE_THETA) * jnp.arange(half, dtype=F32) / half)
    ang = pos.astype(F32)[:, None] * inv_freq[None, :]
    cos, sin = jnp.cos(ang), jnp.sin(ang)
    n = pos.shape[0]
    cos_h = jnp.concatenate([cos, cos, jnp.ones((n, width - 2 * half), F32)], axis=-1)
    sin_h = jnp.concatenate([-sin, sin, jnp.zeros((n, width - 2 * half), F32)], axis=-1)
    return jnp.tile(cos_h, (1, reps)), jnp.tile(sin_h, (1, reps))


def _new_page_t(a):
    return jnp.pad(jnp.swapaxes(a, 1, 2), ((0, 0), (0, 0), (0, PAGE_SIZE - a.shape[1])))


def kernel(x_prompt, x_sample, cache_fox_k, cache_fox_v, cache_fox_logf, cache_mla_ckv, cache_mla_krope,
           cache_dsa_k, cache_dsa_v, cache_dsa_kidx, cache_mem_k, cache_mem_v, page_table, mem_prompt,
           w_in, b_forget, mla_q_norm, mla_kv_norm, w_mla_q_up, w_mla_kv_up, w_mem_kv, w_branch, w_out,
           ln_gain, ln_bias):
    b, t, d = x_prompt.shape
    bs, ts, _ = x_sample.shape
    depth = w_in.shape[0]
    mem_len = mem_prompt.shape[1]
    npages = page_table.shape[1]
    past_len = npages * PAGE_SIZE
    pool = cache_fox_k.shape[0]
    rows = ts * N_HEADS
    alpha = (2 * depth) ** 0.25
    assert FOX_HEADS == MLA_HEADS == DSA_HEADS == IDX_HEADS == N_HEADS and ts <= PAGE_SIZE

    tq = min(128, t)
    tq_dsa = min(256, t)
    tk = min(512, t)
    tk_mla = min(256, t)
    tr = min(256, t)
    pg = math.gcd(npages, 16)
    n_s = bs * ts
    topk_s = min(TOPK_MAX, (past_len + ts) // 4)

    w_pad = _pad_cols(w_in).astype(BF16)
    bf_pad = jnp.pad(b_forget, ((0, 0), (0, LANES - FOX_HEADS)))[:, None, :]
    qn_pad = jnp.pad(mla_q_norm, ((0, 0), (0, 512 - MLA_Q_LORA)))[:, None, :]
    kvn = mla_kv_norm[:, None, :]
    wq = w_mla_q_up.reshape(depth, MLA_Q_LORA, MLA_HEADS, MLA_NOPE + MLA_ROPE)
    wq_pad = lambda a: jnp.pad(a, ((0, 0), (0, 512 - MLA_Q_LORA), (0, 0), (0, LANES - a.shape[-1]))
                               ).reshape(depth, 512, MLA_HEADS * LANES).astype(BF16)
    wq_nope, wq_rope = wq_pad(wq[..., :MLA_NOPE]), wq_pad(wq[..., MLA_NOPE:])
    wkt = jnp.pad(jnp.transpose(w_mla_kv_up[..., :MLA_NOPE], (0, 2, 3, 1)), ((0, 0), (0, 0), (0, LANES - MLA_NOPE), (0, 0))
                  ).reshape(depth, MLA_HEADS * LANES, MLA_KV_LORA).astype(BF16)
    wv = jnp.transpose(w_mla_kv_up[..., MLA_NOPE:], (0, 2, 1, 3))
    wv_slot = jnp.stack([jnp.pad(wv[:, h], ((0, 0), (0, 0), ((h % 2) * MLA_V, (1 - h % 2) * MLA_V)))
                         for h in range(MLA_HEADS)], axis=1).astype(BF16)
    wv_all = jnp.transpose(wv, (0, 2, 1, 3)).reshape(depth, MLA_KV_LORA, MLA_W).astype(BF16)
    w_mem = w_mem_kv.astype(BF16)
    wb, wo = w_branch.astype(BF16), w_out.astype(BF16)
    ln_g, ln_b = ln_gain[:, None, :], ln_bias[:, None, :]

    pos_p = jnp.arange(t, dtype=jnp.int32)
    pos_s = jnp.tile(past_len + jnp.arange(ts, dtype=jnp.int32), bs)
    tabs_p = _rot_tables(pos_p, ROT_DIM // 2, HEAD_DIM, 2) + _rot_tables(pos_p, MLA_ROPE // 2, MLA_ROPE, 4)
    tabs_s = _rot_tables(pos_s, ROT_DIM // 2, HEAD_DIM, 2) + _rot_tables(pos_s, MLA_ROPE // 2, MLA_ROPE, 4)

    kv_t = lambda c: jnp.transpose(c, (0, 1, 3, 4, 2)).reshape(pool, depth, FOX_KV_HEADS * HEAD_DIM, PAGE_SIZE)
    c_fkt, c_fvt, c_dkt, c_dvt = kv_t(cache_fox_k), kv_t(cache_fox_v), kv_t(cache_dsa_k), kv_t(cache_dsa_v)
    c_lft = jnp.swapaxes(cache_fox_logf, 2, 3)
    c_krt = jnp.swapaxes(cache_mla_krope, 2, 3)
    c_ixt = jnp.swapaxes(cache_dsa_kidx, 2, 3)
    c_mk = cache_mem_k.reshape(bs, depth, mem_len, MEM_W)
    c_mv = cache_mem_v.reshape(bs, depth, mem_len, MEM_W)

    xp = x_prompt.reshape(b * t, d)
    xs = x_sample.reshape(n_s, d)
    mem2 = mem_prompt.reshape(b * mem_len, d)
    P = {k: [] for k in ('fox_k', 'fox_v', 'logf', 'ckv', 'kr', 'dsa_k', 'dsa_v', 'ik', 'mem_k', 'mem_v')}
    S = {k: [] for k in ('fox_k', 'fox_v', 'logf', 'ckv', 'kr', 'dsa_k', 'dsa_v', 'ik')}

    def col(h2, name, width=None):
        off = _LAYOUT[name][0]
        return h2[:, off:off + (width or _LAYOUT[name][1])]

    for l in range(depth):
        params = (bf_pad[l], qn_pad[l], kvn[l], wq_nope[l], wq_rope[l], wkt[l])

        h2 = _matmul(xp, w_pad[l], min(1024, b * t), 256)
        pr = _prep(h2, tabs_p, params, tr)
        h3 = h2.reshape(b, t, EP)
        r3 = lambda a: a.reshape(b, t, a.shape[-1])
        c, ct = _cumsum(r3(pr['logf']))
        fox_o = _fox_prompt(h3, c, ct, tq, tk)
        mla_o = _mla_prompt(r3(pr['qlat']), r3(pr['qrope']), r3(pr['ckv']), r3(pr['kr']), wv_slot[l], tq, tk_mla)
        dsa_o = _dsa_prompt(r3(pr['iq']), r3(pr['iw']), r3(pr['ik']), r3(pr['dq']), r3(pr['dk']), h3, tq_dsa, tk)
        mkv = _matmul(mem2, w_mem[l], min(1024, b * mem_len), 2 * MEM_W).reshape(b, mem_len, 2 * MEM_W)
        tqm = min(256, t)
        mem_o = _mem_attention(
            h3, pl.BlockSpec((1, tqm, MEM_W), lambda bi, i: (bi, i, _blk('mem_q'))),
            mkv, pl.BlockSpec((1, mem_len, MEM_W), lambda bi, i: (bi, 0, 0)),
            mkv, pl.BlockSpec((1, mem_len, MEM_W), lambda bi, i: (bi, 0, 1)),
            (b, t // tqm), (b, t, MEM_W), pl.BlockSpec((1, tqm, MEM_W), lambda bi, i: (bi, i, 0)))
        xp_new = _finish(xp, h2, fox_o.reshape(b * t, -1), mla_o.reshape(b * t, -1), dsa_o.reshape(b * t, -1),
                         mem_o.reshape(b * t, -1), wb[l], wo[l], ln_g[l], ln_b[l], tr, alpha)
        P['fox_k'].append(col(h2, 'fox_k')); P['fox_v'].append(col(h2, 'fox_v')); P['logf'].append(pr['logf'])
        P['ckv'].append(pr['ckv']); P['kr'].append(pr['kr'][:, :MLA_ROPE]); P['dsa_k'].append(pr['dk'])
        P['dsa_v'].append(col(h2, 'dsa_v')); P['ik'].append(pr['ik'][:, :IDX_DIM])
        P['mem_k'].append(mkv[..., :MEM_W]); P['mem_v'].append(mkv[..., MEM_W:])
        xp = xp_new

        g2 = _matmul(xs, w_pad[l], n_s, 256)
        sr = _prep(g2, tabs_s, params, n_s)
        rows_of = lambda a: a.reshape(bs, rows, a.shape[-1] // N_HEADS)
        new3 = lambda a: a.reshape(bs, ts, a.shape[-1])
        fold = lambda o: jnp.where(((jnp.arange(rows) % N_HEADS) // GROUP == 0)[None, :, None],
                                   o[..., :HEAD_DIM], o[..., HEAD_DIM:]).reshape(n_s, N_HEADS * HEAD_DIM)
        lf_new = new3(sr['logf'])
        lrows = jnp.broadcast_to(jnp.transpose(lf_new, (0, 2, 1))[:, None], (bs, ts, FOX_HEADS, ts)).reshape(bs, rows, ts)
        lrows = jnp.pad(lrows, ((0, 0), (0, 0), (0, LANES - ts)))
        fox_s = fold(_fox_sample(page_table, rows_of(col(g2, 'fox_q')), lrows,
                                 _new_page_t(new3(col(g2, 'fox_k'))), _new_page_t(new3(col(g2, 'fox_v'))),
                                 _new_page_t(lf_new), c_fkt, c_fvt, c_lft, l, pg))
        olat = _mla_sample(page_table, rows_of(sr['qlat']), rows_of(sr['qrope'])[..., :MLA_ROPE],
                           _new_page_t(new3(sr['ckv'])).swapaxes(1, 2), _new_page_t(new3(sr['kr'][:, :MLA_ROPE])),
                           cache_mla_ckv, c_krt, l, pg)
        mla_s = _mla_up(olat, wv_all[l]).reshape(bs, rows, MLA_HEADS, MLA_V).sum(axis=2).reshape(n_s, MLA_W)
        bias = _dsa_idx_sample(page_table, rows_of(sr['iq'])[..., :IDX_DIM], sr['iw'].reshape(bs, rows, 1),
                               _new_page_t(new3(sr['ik'][:, :IDX_DIM])), c_ixt, l, pg, topk_s)
        dsa_s = fold(_dsa_sample(page_table, rows_of(sr['dq']), _new_page_t(new3(sr['dk'])),
                                 _new_page_t(new3(col(g2, 'dsa_v'))), bias, c_dkt, c_dvt, l, pg))
        mem_s = _mem_attention(
            new3(col(g2, 'mem_q')), pl.BlockSpec((1, ts, MEM_W), lambda bi: (bi, 0, 0)),
            c_mk, pl.BlockSpec((None, None, mem_len, MEM_W), lambda bi, l=l: (bi, l, 0, 0)),
            c_mv, pl.BlockSpec((None, None, mem_len, MEM_W), lambda bi, l=l: (bi, l, 0, 0)),
            (bs,), (bs, ts, MEM_W), pl.BlockSpec((1, ts, MEM_W), lambda bi: (bi, 0, 0))).reshape(n_s, MEM_W)
        xs_new = _finish(xs, g2, fox_s, mla_s, dsa_s, mem_s, wb[l], wo[l], ln_g[l], ln_b[l], n_s, alpha)
        S['fox_k'].append(col(g2, 'fox_k')); S['fox_v'].append(col(g2, 'fox_v')); S['logf'].append(sr['logf'])
        S['ckv'].append(sr['ckv']); S['kr'].append(sr['kr'][:, :MLA_ROPE]); S['dsa_k'].append(sr['dk'])
        S['dsa_v'].append(col(g2, 'dsa_v')); S['ik'].append(sr['ik'][:, :IDX_DIM])
        xs = xs_new

    def stack(lst, lead, tail):
        return jnp.stack([a.reshape(lead + tail) for a in lst], axis=1)

    kv2 = (FOX_KV_HEADS, HEAD_DIM)
    pl_, sl_ = (b, t), (bs, ts)
    return (xp.reshape(b, t, d), xs.reshape(bs, ts, d),
            stack(P['fox_k'], pl_, kv2), stack(P['fox_v'], pl_, kv2), stack(P['logf'], pl_, (FOX_HEADS,)),
            stack(P['ckv'], pl_, (MLA_KV_LORA,)), stack(P['kr'], pl_, (MLA_ROPE,)),
            stack(P['dsa_k'], pl_, kv2), stack(P['dsa_v'], pl_, kv2), stack(P['ik'], pl_, (IDX_DIM,)),
            stack(P['mem_k'], (b, mem_len), (MEM_HEADS, HEAD_DIM)), stack(P['mem_v'], (b, mem_len), (MEM_HEADS, HEAD_DIM)),
            stack(S['fox_k'], sl_, kv2), stack(S['fox_v'], sl_, kv2), stack(S['logf'], sl_, (FOX_HEADS,)),
            stack(S['ckv'], sl_, (MLA_KV_LORA,)), stack(S['kr'], sl_, (MLA_ROPE,)),
            stack(S['dsa_k'], sl_, kv2), stack(S['dsa_v'], sl_, kv2), stack(S['ik'], sl_, (IDX_DIM,)))
```

```python
import functools
import math

import numpy as np
import jax
import jax.numpy as jnp
from jax import lax
from jax.experimental import pallas as pl
from jax.experimental.pallas import tpu as pltpu

F32 = jnp.float32
BF16 = jnp.bfloat16
I32 = jnp.int32

HEAD_DIM = 64
ROT_DIM = HEAD_DIM // 4
ROPE_THETA = 500000.0
FOX_HEADS = 8
FOX_KV_HEADS = 2
MLA_HEADS = 8
MLA_NOPE = 64
MLA_ROPE = 32
MLA_V = 64
MLA_Q_LORA = 384
MLA_KV_LORA = 256
DSA_HEADS = 8
DSA_KV_HEADS = 2
IDX_HEADS = 8
IDX_DIM = 64
TOPK_MAX = 256
MEM_HEADS = 4
N_BRANCH = 4
PAGE_SIZE = 128
N_HEADS = 8
GROUP = N_HEADS // FOX_KV_HEADS
FOX_W = FOX_HEADS * HEAD_DIM
MLA_W = MLA_HEADS * MLA_V
DSA_W = DSA_HEADS * HEAD_DIM
MEM_W = MEM_HEADS * HEAD_DIM
BRANCH_WIDTHS = (FOX_W, MLA_W, DSA_W, MEM_W)
FOX_SCALE = HEAD_DIM ** -0.5
MLA_SCALE = (MLA_NOPE + MLA_ROPE) ** -0.5
DSA_SCALE = HEAD_DIM ** -0.5
MEM_SCALE = HEAD_DIM ** -0.5
IDX_SCALE = IDX_DIM ** -0.5
IDX_W_SCALE = IDX_HEADS ** -0.5
LN_EPS = 1e-5
RMS_EPS = 1e-6

LANES = 128
SUBLANES = 8
VMEM_LIMIT_BYTES = 48 * 2 ** 20

NEG = -1e30
INT_MIN = -2 ** 31

_IN_NAMES = ('fox_q', 'fox_k', 'fox_v', 'fox_f', 'fox_z', 'mla_qa', 'mla_kva', 'mla_kr', 'mla_z',
             'dsa_q', 'dsa_k', 'dsa_v', 'idx_q', 'idx_k', 'idx_w', 'dsa_z', 'mem_q', 'mem_z', 'gates')
_IN_SPLITS = (512, 128, 128, 8, 512, 384, 256, 32, 512, 512, 128, 128, 512, 64, 8, 512, 256, 256, 4096)
_SRC = dict(zip(_IN_NAMES, np.cumsum((0,) + _IN_SPLITS[:-1]).tolist()))
_WID = dict(zip(_IN_NAMES, _IN_SPLITS))
_GROUPS = (
    ('gates', 4096, None),
    ('fox_q', 1024, 'kv'), ('dsa_q', 1024, 'kv'), ('idx_q', 1024, 'lo'),
    ('fox_z', 512, None), ('mla_z', 512, None), ('dsa_z', 512, None), ('mla_qa', 512, None),
    ('mla_kva', 256, None), ('mem_q', 256, None), ('mem_z', 256, None),
    ('fox_k', 128, None), ('fox_v', 128, None), ('dsa_k', 128, None), ('dsa_v', 128, None),
    ('idx_k', 128, None), ('mla_kr', 128, None), ('fox_f', 128, None), ('idx_w', 128, None),
    ('pad', 256, None),
)
PROJ_TN = 1024


def _layout():
    off, out = 0, {}
    for name, pad, _ in _GROUPS:
        assert off % pad == 0
        out[name] = (off, pad)
        off += pad
    return out, off


_LAYOUT, EP = _layout()


def _blk(name):
    off, pad = _LAYOUT[name]
    return off // pad


def _cparams(n_grid):
    return pltpu.CompilerParams(dimension_semantics=('arbitrary',) * n_grid, vmem_limit_bytes=VMEM_LIMIT_BYTES)


def _dot(a, b):
    return jnp.dot(a, b, preferred_element_type=F32)


def _dot_nt(a, b):
    return lax.dot_general(a, b, (((1,), (1,)), ((), ())), preferred_element_type=F32)


def _dot_tn(a, b):
    return lax.dot_general(a, b, (((0,), (0,)), ((), ())), preferred_element_type=F32)


def _split3(x):
    hi = x.astype(BF16)
    r = x - hi.astype(F32)
    mid = r.astype(BF16)
    lo = (r - mid.astype(F32)).astype(BF16)
    return hi, mid, lo


def _iota(shape, dim):
    return lax.broadcasted_iota(I32, shape, dim)


def _rows(blocks):
    return jnp.concatenate(blocks, axis=0)


def _mm_kernel(x_ref, w_ref, o_ref, xb_ref):
    @pl.when(pl.program_id(1) == 0)
    def _():
        xb_ref[...] = x_ref[...].astype(BF16)

    o_ref[...] = _dot(xb_ref[...], w_ref[...])


def _matmul(x, w, tm, tn):
    m, k = x.shape
    n = w.shape[1]
    assert m % tm == 0 and n % tn == 0 and w.dtype == BF16
    return pl.pallas_call(
        _mm_kernel,
        grid=(m // tm, n // tn),
        in_specs=[pl.BlockSpec((tm, k), lambda i, j: (i, 0)), pl.BlockSpec((k, tn), lambda i, j: (0, j))],
        out_specs=pl.BlockSpec((tm, tn), lambda i, j: (i, j)),
        out_shape=jax.ShapeDtypeStruct((m, n), F32),
        scratch_shapes=[pltpu.VMEM((tm, k), BF16)],
        compiler_params=_cparams(2),
        name='proj_matmul',
    )(x, w)


def _rope(x, cos, sin, half, width):
    w = x.shape[-1]
    d = _iota(x.shape, 1) % width
    up = pltpu.roll(x, w - half, 1)
    dn = pltpu.roll(x, half, 1)
    partner = jnp.where(d < half, up, dn)
    return jnp.where(d < 2 * half, x * cos + partner * sin, x)


def _tile_lanes(x, n):
    return jnp.concatenate([x] * n, axis=-1) if n > 1 else x


def _prep_kernel(qa_ref, kva_ref, dq_ref, dk_ref, iq_ref, ik_ref, kr_ref, ff_ref, iw_ref,
                 ca_ref, sa_ref, cb_ref, sb_ref, bf_ref, qn_ref, kvn_ref, wqn_ref, wqr_ref, wkt_ref,
                 logf_ref, ckv_ref, kro_ref, dqo_ref, dko_ref, iqo_ref, iko_ref, iwo_ref, qlat_ref, qrope_ref):
    ca, sa, cb, sb = ca_ref[...], sa_ref[...], cb_ref[...], sb_ref[...]
    half_a, half_b = ROT_DIM // 2, MLA_ROPE // 2
    logf_ref[...] = jax.nn.log_sigmoid(ff_ref[...] + bf_ref[...])[:, :FOX_HEADS]
    kva = kva_ref[...]
    ckv_ref[...] = kva * lax.rsqrt(jnp.mean(kva * kva, axis=-1, keepdims=True) + RMS_EPS) * kvn_ref[...]
    kro_ref[...] = _rope(kr_ref[...], cb, sb, half_b, MLA_ROPE)
    dqo_ref[...] = (_rope(dq_ref[...], _tile_lanes(ca, N_HEADS), _tile_lanes(sa, N_HEADS), half_a, HEAD_DIM)
                    * DSA_SCALE).astype(BF16)
    dko_ref[...] = _rope(dk_ref[...], ca, sa, half_a, HEAD_DIM)
    iqo_ref[...] = (_rope(iq_ref[...], _tile_lanes(ca, N_HEADS), _tile_lanes(sa, N_HEADS), half_a, IDX_DIM)
                    * IDX_SCALE).astype(BF16)
    iko_ref[...] = _rope(ik_ref[...], ca, sa, half_a, IDX_DIM)
    iwo_ref[...] = (iw_ref[...] * IDX_W_SCALE)[:, :IDX_HEADS]
    qa = qa_ref[...]
    ms = jnp.sum(qa * qa, axis=-1, keepdims=True) * (1.0 / MLA_Q_LORA)
    qan = (qa * lax.rsqrt(ms + RMS_EPS) * qn_ref[...]).astype(BF16)
    q_nope = _dot(qan, wqn_ref[...]).astype(BF16)
    q_rope = _dot(qan, wqr_ref[...])
    qrope_ref[...] = (_rope(q_rope, _tile_lanes(cb, N_HEADS), _tile_lanes(sb, N_HEADS), half_b, MLA_ROPE)
                      * MLA_SCALE).astype(BF16)
    for h in range(MLA_HEADS):
        qlat_ref[:, MLA_KV_LORA * h:MLA_KV_LORA * (h + 1)] = (_dot(
            q_nope[:, LANES * h:LANES * (h + 1)], wkt_ref[LANES * h:LANES * (h + 1), :]) * MLA_SCALE).astype(BF16)


def _prep(h2, tabs, params, tr):
    n = h2.shape[0]
    nt = tabs[0].shape[0] // tr
    row = lambda name: pl.BlockSpec((tr, _LAYOUT[name][1]), lambda i, name=name: (i, _blk(name)))
    tab = pl.BlockSpec((tr, LANES), lambda i: (i % nt, 0))
    full = lambda a: pl.BlockSpec(a.shape, lambda i: (0,) * a.ndim)
    names = ('mla_qa', 'mla_kva', 'dsa_q', 'dsa_k', 'idx_q', 'idx_k', 'mla_kr', 'fox_f', 'idx_w')
    in_specs = [row(nm) for nm in names] + [tab] * 4 + [full(p) for p in params]
    outs = [('logf', FOX_HEADS), ('ckv', MLA_KV_LORA), ('kr', LANES), ('dq', N_HEADS * LANES), ('dk', LANES),
            ('iq', N_HEADS * LANES), ('ik', LANES), ('iw', IDX_HEADS), ('qlat', MLA_HEADS * MLA_KV_LORA),
            ('qrope', N_HEADS * LANES)]
    res = pl.pallas_call(
        _prep_kernel,
        grid=(n // tr,),
        in_specs=in_specs,
        out_specs=[pl.BlockSpec((tr, w), lambda i: (i, 0)) for _, w in outs],
        out_shape=[jax.ShapeDtypeStruct((n, w), BF16 if k in ('dq', 'iq', 'qlat', 'qrope') else F32) for k, w in outs],
        compiler_params=_cparams(1),
        name='prep_rows',
    )(*([h2] * len(names)), *tabs, *params)
    return dict(zip([k for k, _ in outs], res))


def _cumsum_kernel(lf_ref, c_ref, ct_ref, *, blk):
    t = lf_ref.shape[1]
    r, c = _iota((blk, blk), 0), _iota((blk, blk), 1)
    tri = jnp.where(c <= r, 1.0, 0.0).astype(BF16)
    trit = jnp.where(r <= c, 1.0, 0.0).astype(BF16)
    carry = jnp.zeros((1, FOX_HEADS), F32)
    carry_t = jnp.zeros((FOX_HEADS, 1), F32)
    for i in range(t // blk):
        parts = _split3(lf_ref[0, blk * i:blk * (i + 1), :])
        cb = sum(_dot(tri, p) for p in parts) + carry
        cbt = sum(_dot_tn(p, trit) for p in parts) + carry_t
        c_ref[0, blk * i:blk * (i + 1), :] = cb
        ct_ref[0, :, blk * i:blk * (i + 1)] = cbt
        carry = cb[blk - 1:blk, :]
        carry_t = cbt[:, blk - 1:blk]


def _cumsum(lf):
    b, t, h = lf.shape
    return pl.pallas_call(
        functools.partial(_cumsum_kernel, blk=min(256, t)),
        grid=(b,),
        in_specs=[pl.BlockSpec((1, t, h), lambda i: (i, 0, 0))],
        out_specs=[pl.BlockSpec((1, t, h), lambda i: (i, 0, 0)), pl.BlockSpec((1, h, t), lambda i: (i, 0, 0))],
        out_shape=[jax.ShapeDtypeStruct((b, t, h), F32), jax.ShapeDtypeStruct((b, h, t), F32)],
        compiler_params=_cparams(1),
        name='forget_cumsum',
    )(lf)


def _online(state, s, pv):
    m_i, l_i, acc = state
    m_new = jnp.maximum(m_i, jnp.max(s, axis=-1, keepdims=True))
    alpha = jnp.exp(m_i - m_new)
    p = jnp.exp(s - m_new)
    l_new = alpha * l_i + jnp.sum(p, axis=-1, keepdims=True)
    acc_new = alpha * acc + pv(p.astype(BF16))
    return m_new, l_new, acc_new


def _init_state(rows, dv):
    return (jnp.full((rows, 1), NEG, F32), jnp.zeros((rows, 1), F32), jnp.zeros((rows, dv), F32))


def _causal_rows(s, i, j, tq, tk):
    qpos = i * tq + _iota(s.shape, 0) % tq
    kpos = j * tk + _iota(s.shape, 1)
    return jnp.where(kpos <= qpos, s, NEG)


def _compact_pairs(o, g, tq, o_ref, first_pair):
    lo = _iota((tq, LANES), 1) < HEAD_DIM
    for m in range(GROUP // 2):
        a, b = o[2 * m * tq:(2 * m + 1) * tq], o[(2 * m + 1) * tq:(2 * m + 2) * tq]
        pair = jnp.where(lo, a, pltpu.roll(b, HEAD_DIM, 1)) if g == 0 else jnp.where(lo, pltpu.roll(a, HEAD_DIM, 1), b)
        o_ref[0, :, LANES * (first_pair + m):LANES * (first_pair + m + 1)] = pair


def _fox_p_kernel(q_ref, k_ref, v_ref, c_ref, ct_ref, o_ref, *, tq, tk, nc):
    i = pl.program_id(1)
    nfull = (i * tq) // tk
    for g in range(FOX_KV_HEADS):
        heads = range(GROUP * g, GROUP * (g + 1))
        qg = _rows([(q_ref[0, :, LANES * h:LANES * (h + 1)] * FOX_SCALE).astype(BF16) for h in heads])
        cq = _rows([c_ref[0, :, h:h + 1] for h in heads])

        def chunk(j, st, masked, qg=qg, cq=cq, heads=heads):
            kc = k_ref[0, pl.ds(j * tk, tk), :].astype(BF16)
            vc = v_ref[0, pl.ds(j * tk, tk), :].astype(BF16)
            ck = _rows([jnp.broadcast_to(ct_ref[0, pl.ds(h * nc + j, 1), :], (tq, tk)) for h in heads])
            s = _dot_nt(qg, kc) + (cq - ck)
            if masked:
                s = _causal_rows(s, i, j, tq, tk)
            return _online(st, s, lambda p: _dot(p, vc))

        st = lax.fori_loop(0, nfull, functools.partial(chunk, masked=False), _init_state(GROUP * tq, LANES))
        _, l_i, acc = chunk(nfull, st, True)
        _compact_pairs(acc / l_i, g, tq, o_ref, g * (GROUP // 2))


def _fox_prompt(h3, c, ct, tq, tk):
    b, t, _ = h3.shape
    nc = t // tk
    ctr = ct.reshape(b, FOX_HEADS * nc, tk)
    return pl.pallas_call(
        functools.partial(_fox_p_kernel, tq=tq, tk=tk, nc=nc),
        grid=(b, t // tq),
        in_specs=[pl.BlockSpec((1, tq, N_HEADS * LANES), lambda bi, i: (bi, i, _blk('fox_q'))),
                  pl.BlockSpec((1, t, LANES), lambda bi, i: (bi, 0, _blk('fox_k'))),
                  pl.BlockSpec((1, t, LANES), lambda bi, i: (bi, 0, _blk('fox_v'))),
                  pl.BlockSpec((1, tq, FOX_HEADS), lambda bi, i: (bi, i, 0)),
                  pl.BlockSpec((1, FOX_HEADS * nc, tk), lambda bi, i: (bi, 0, 0))],
        out_specs=pl.BlockSpec((1, tq, FOX_W), lambda bi, i: (bi, i, 0)),
        out_shape=jax.ShapeDtypeStruct((b, t, FOX_W), F32),
        compiler_params=_cparams(2),
        name='fox_prompt',
    )(h3, h3, h3, c, ctr)


def _mla_p_kernel(ql_ref, qr_ref, ckv_ref, kr_ref, wv_ref, o_ref, *, tq, tk):
    i = pl.program_id(1)
    nfull = (i * tq) // tk
    ql = _rows([ql_ref[0, :, MLA_KV_LORA * h:MLA_KV_LORA * (h + 1)].astype(BF16) for h in range(MLA_HEADS)])
    qr = _rows([qr_ref[0, :, LANES * h:LANES * (h + 1)].astype(BF16) for h in range(MLA_HEADS)])

    def chunk(j, st, masked):
        kc = ckv_ref[0, pl.ds(j * tk, tk), :].astype(BF16)
        kr = kr_ref[0, pl.ds(j * tk, tk), :].astype(BF16)
        s = _dot_nt(ql, kc) + _dot_nt(qr, kr)
        if masked:
            s = _causal_rows(s, i, j, tq, tk)
        return _online(st, s, lambda p: _dot(p, kc))

    st = lax.fori_loop(0, nfull, functools.partial(chunk, masked=False), _init_state(MLA_HEADS * tq, MLA_KV_LORA))
    _, l_i, acc = chunk(nfull, st, True)
    o = (acc / l_i).astype(BF16)
    for m in range(MLA_HEADS // 2):
        o_ref[0, :, LANES * m:LANES * (m + 1)] = (_dot(o[2 * m * tq:(2 * m + 1) * tq], wv_ref[2 * m])
                                                  + _dot(o[(2 * m + 1) * tq:(2 * m + 2) * tq], wv_ref[2 * m + 1]))


def _mla_prompt(qlat, qrope, ckv, kr, wv_slot, tq, tk):
    b, t, _ = qlat.shape
    return pl.pallas_call(
        functools.partial(_mla_p_kernel, tq=tq, tk=tk),
        grid=(b, t // tq),
        in_specs=[pl.BlockSpec((1, tq, qlat.shape[2]), lambda bi, i: (bi, i, 0)),
                  pl.BlockSpec((1, tq, qrope.shape[2]), lambda bi, i: (bi, i, 0)),
                  pl.BlockSpec((1, t, MLA_KV_LORA), lambda bi, i: (bi, 0, 0)),
                  pl.BlockSpec((1, t, LANES), lambda bi, i: (bi, 0, 0)),
                  pl.BlockSpec(wv_slot.shape, lambda bi, i: (0, 0, 0))],
        out_specs=pl.BlockSpec((1, tq, MLA_W), lambda bi, i: (bi, i, 0)),
        out_shape=jax.ShapeDtypeStruct((b, t, MLA_W), F32),
        compiler_params=_cparams(2),
        name='mla_prompt',
    )(qlat, qrope, ckv, kr, wv_slot)


def _sort_key(score):
    b = lax.bitcast_convert_type(jnp.where(score == 0.0, 0.0, score), I32)
    return b ^ ((b >> 31) & 0x7FFFFFFF)


def _select_bias(keys_ref, write, topk, n_used):
    nc, rows, tk = keys_ref.shape
    kf = float(topk)
    rb = min(rows, 128)

    def counts(preds):
        out = []
        for r0 in range(0, rows, rb):
            def body(j, accs, r0=r0):
                x = keys_ref[j, r0:r0 + rb, :]
                new = []
                for acc, pred in zip(accs, preds):
                    w = jnp.where(pred(x, j, r0), 1.0, 0.0)
                    new.append(acc + sum(w[:, LANES * c:LANES * (c + 1)] for c in range(tk // LANES)))
                return tuple(new)
            accs = lax.fori_loop(0, n_used, body, tuple(jnp.zeros((rb, LANES), F32) for _ in preds))
            out.append([jnp.sum(a, axis=-1, keepdims=True) for a in accs])
        return [_rows([o[n] for o in out]) for n in range(len(preds))]

    def count(pred):
        return counts([pred])[0]

    def ge(cand):
        return lambda x, j, r0: x >= cand[r0:r0 + rb]

    def refine(cur, c1, c2, c3):
        n1, n2, n3 = counts([ge(c1), ge(c2), ge(c3)])
        return jnp.where(n3 >= kf, c3, jnp.where(n2 >= kf, c2, jnp.where(n1 >= kf, c1, cur)))

    low = jnp.full((rows, 1), INT_MIN, I32)
    b30 = jnp.int32(1 << 30)
    cur = refine(low, low | b30, jnp.zeros((rows, 1), I32), jnp.zeros((rows, 1), I32) | b30)

    def vbody(b, cur):
        hi = jnp.left_shift(jnp.int32(1), 29 - 2 * b)
        lo = jnp.left_shift(jnp.int32(1), 28 - 2 * b)
        return refine(cur, cur | lo, cur | hi, cur | hi | lo)

    thr = lax.fori_loop(0, 15, vbody, cur)
    eq = lambda x, j, r0: x == thr[r0:r0 + rb]
    n_gt, n_eq = counts([lambda x, j, r0: x > thr[r0:r0 + rb], eq])
    need = kf - n_gt
    lane = _iota((rb, tk), 1)
    nbits = max(1, int(math.ceil(math.log2(nc * tk))))

    def tie_break():
        def ibody(b, p):
            cand = p + jnp.left_shift(jnp.int32(1), nbits - 1 - b)
            n_lt = count(lambda x, j, r0: (x == thr[r0:r0 + rb]) & (j * tk + lane < cand[r0:r0 + rb]))
            return jnp.where(n_lt < need, cand, p)
        return lax.fori_loop(0, nbits, ibody, jnp.zeros((rows, 1), I32))

    excess = jnp.max(jnp.where(thr > INT_MIN, n_eq - need, 0.0))
    p = lax.cond(excess > 0.0, tie_break, lambda: jnp.full((rows, 1), nc * tk, I32))
    lane_all = _iota((rows, tk), 1)

    def wbody(j, carry):
        x = keys_ref[j]
        sel = (x > INT_MIN) & ((x > thr) | ((x == thr) & (j * tk + lane_all <= p)))
        write(j, jnp.where(sel, 0.0, NEG))
        return carry

    lax.fori_loop(0, n_used, wbody, 0)


def _dsa_p_kernel(iq_ref, iw_ref, ik_ref, q_ref, k_ref, v_ref, o_ref, keys_ref, bias_ref, *, tq, tk, topk):
    i = pl.program_id(1)
    n_used = (i * tq) // tk + 1
    iq = _rows([iq_ref[0, :, LANES * h:LANES * (h + 1)].astype(BF16) for h in range(IDX_HEADS)])
    iw = _rows([iw_ref[0, :, h:h + 1] for h in range(IDX_HEADS)])

    def ibody(j, carry):
        w = jnp.maximum(_dot_nt(iq, ik_ref[0, pl.ds(j * tk, tk), :].astype(BF16)), 0.0) * iw
        score = w[0:tq]
        for h in range(1, IDX_HEADS):
            score = score + w[h * tq:(h + 1) * tq]
        qpos = i * tq + _iota((tq, tk), 0)
        kpos = j * tk + _iota((tq, tk), 1)
        keys_ref[j] = jnp.where(kpos <= qpos, _sort_key(score), INT_MIN)
        return carry

    lax.fori_loop(0, n_used, ibody, 0)

    def write(j, bias):
        bias_ref[j] = bias

    _select_bias(keys_ref, write, topk, n_used)

    for g in range(DSA_KV_HEADS):
        qg = _rows([q_ref[0, :, LANES * h:LANES * (h + 1)].astype(BF16) for h in range(GROUP * g, GROUP * (g + 1))])

        def chunk(j, st, qg=qg):
            kc = k_ref[0, pl.ds(j * tk, tk), :].astype(BF16)
            vc = v_ref[0, pl.ds(j * tk, tk), :].astype(BF16)
            bias = bias_ref[j]
            return _online(st, _dot_nt(qg, kc) + _rows([bias] * GROUP), lambda p: _dot(p, vc))

        _, l_i, acc = lax.fori_loop(0, n_used, chunk, _init_state(GROUP * tq, LANES))
        _compact_pairs(acc / l_i, g, tq, o_ref, g * (GROUP // 2))


def _dsa_prompt(iq, iw, ik, dq, dk, h3, tq, tk):
    b, t, _ = dq.shape
    nc = t // tk
    topk = min(TOPK_MAX, t // 4)
    return pl.pallas_call(
        functools.partial(_dsa_p_kernel, tq=tq, tk=tk, topk=topk),
        grid=(b, t // tq),
        in_specs=[pl.BlockSpec((1, tq, N_HEADS * LANES), lambda bi, i: (bi, i, 0)),
                  pl.BlockSpec((1, tq, IDX_HEADS), lambda bi, i: (bi, i, 0)),
                  pl.BlockSpec((1, t, LANES), lambda bi, i: (bi, 0, 0)),
                  pl.BlockSpec((1, tq, N_HEADS * LANES), lambda bi, i: (bi, i, 0)),
                  pl.BlockSpec((1, t, LANES), lambda bi, i: (bi, 0, 0)),
                  pl.BlockSpec((1, t, LANES), lambda bi, i: (bi, 0, _blk('dsa_v')))],
        out_specs=pl.BlockSpec((1, tq, DSA_W), lambda bi, i: (bi, i, 0)),
        out_shape=jax.ShapeDtypeStruct((b, t, DSA_W), F32),
        scratch_shapes=[pltpu.VMEM((nc, tq, tk), I32), pltpu.VMEM((nc, tq, tk), F32)],
        compiler_params=_cparams(2),
        name='dsa_prompt',
    )(iq, iw, ik, dq, dk, h3)


def _mem_kernel(q_ref, k_ref, v_ref, o_ref):
    tq = q_ref.shape[1]
    half = _iota((k_ref.shape[-2], LANES), 1) // HEAD_DIM
    for m in range(MEM_HEADS // 2):
        qp = q_ref[0, :, LANES * m:LANES * (m + 1)].astype(BF16)
        kp = k_ref[:, LANES * m:LANES * (m + 1)] if k_ref.ndim == 2 else k_ref[0, :, LANES * m:LANES * (m + 1)]
        vp = v_ref[:, LANES * m:LANES * (m + 1)] if v_ref.ndim == 2 else v_ref[0, :, LANES * m:LANES * (m + 1)]
        pair = jnp.zeros((tq, LANES), F32)
        for s in range(2):
            kc = jnp.where(half == s, kp, 0.0).astype(BF16)
            vc = jnp.where(half == s, vp, 0.0).astype(BF16)
            sc = _dot_nt(qp, kc) * MEM_SCALE
            p = jnp.exp(sc - jnp.max(sc, axis=-1, keepdims=True))
            p = p / jnp.sum(p, axis=-1, keepdims=True)
            pair = pair + _dot(p.astype(BF16), vc)
        o_ref[0, :, LANES * m:LANES * (m + 1)] = pair


def _mem_attention(q_arr, q_spec, k_arr, k_spec, v_arr, v_spec, grid, out_shape, out_spec):
    return pl.pallas_call(
        _mem_kernel, grid=grid, in_specs=[q_spec, k_spec, v_spec], out_specs=out_spec,
        out_shape=jax.ShapeDtypeStruct(out_shape, F32), compiler_params=_cparams(len(grid)), name='mem_attention',
    )(q_arr, k_arr, v_arr)


def _finish_kernel(x_ref, fo_ref, mo_ref, do_ref, eo_ref, fz_ref, mz_ref, dz_ref, ez_ref, gt_ref,
                   wb_ref, wo_ref, g_ref, b_ref, o_ref, *, alpha):
    offs = np.cumsum((0,) + BRANCH_WIDTHS).tolist()
    d = x_ref.shape[-1]
    y = jnp.zeros(x_ref.shape, F32)
    for n, (o_r, z_r) in enumerate(((fo_ref, fz_ref), (mo_ref, mz_ref), (do_ref, dz_ref), (eo_ref, ez_ref))):
        z = z_r[...]
        u = o_r[...] * (z * jax.nn.sigmoid(z))
        pb = _dot(u.astype(BF16), wb_ref[offs[n]:offs[n + 1], :])
        y = y + jax.nn.sigmoid(gt_ref[:, d * n:d * (n + 1)]) * pb
    r = alpha * x_ref[...] + _dot(y.astype(BF16), wo_ref[...])
    mu = jnp.mean(r, axis=-1, keepdims=True)
    var = jnp.mean(jnp.square(r - mu), axis=-1, keepdims=True)
    o_ref[...] = (r - mu) * lax.rsqrt(var + LN_EPS) * g_ref[...] + b_ref[...]


def _finish(x2, h2, fo, mo, do, eo, wb, wo, g, b, tr, alpha):
    n, d = x2.shape
    row = lambda w: pl.BlockSpec((tr, w), lambda i: (i, 0))
    hcol = lambda name: pl.BlockSpec((tr, _LAYOUT[name][1]), lambda i, name=name: (i, _blk(name)))
    full = lambda a: pl.BlockSpec(a.shape, lambda i: (0,) * a.ndim)
    return pl.pallas_call(
        functools.partial(_finish_kernel, alpha=alpha),
        grid=(n // tr,),
        in_specs=[row(d), row(FOX_W), row(MLA_W), row(DSA_W), row(MEM_W),
                  hcol('fox_z'), hcol('mla_z'), hcol('dsa_z'), hcol('mem_z'), hcol('gates'),
                  full(wb), full(wo), full(g), full(b)],
        out_specs=row(d),
        out_shape=jax.ShapeDtypeStruct((n, d), F32),
        compiler_params=_cparams(1),
        name='finish',
    )(x2, fo, mo, do, eo, h2, h2, h2, h2, h2, wb, wo, g, b)


def _paged_fetch(pt_ref, srcs, bufs, sem, layer, pg, page_of):
    b, p = pl.program_id(0), pl.program_id(1)
    nb, steps = pl.num_programs(0), pl.num_programs(1)
    step = b * steps + p
    slot = step % 2

    def copies(bb, pp, sl):
        return [pltpu.make_async_copy(src.at[pt_ref[bb, page_of(pp, j)], layer], buf.at[sl, j], sem.at[n, sl])
                for n, (src, buf) in enumerate(zip(srcs, bufs)) for j in range(pg)]

    @pl.when(step == 0)
    def _():
        for c in copies(b, p, slot):
            c.start()

    @pl.when(step + 1 < nb * steps)
    def _():
        wrap = p + 1 == steps
        for c in copies(jnp.where(wrap, b + 1, b), jnp.where(wrap, 0, p + 1), 1 - slot):
            c.start()

    for c in copies(b, p, slot):
        c.wait()
    return [[buf.at[slot, j] for j in range(pg)] for buf in bufs]


def _page_scratch(caches, pg):
    return ([pltpu.VMEM((2, pg) + tuple(c.shape[2:]), c.dtype) for c in caches]
            + [pltpu.SemaphoreType.DMA((len(caches), 2))])


_HBM = pl.BlockSpec(memory_space=pl.ANY)


def _per_b(a):
    return pl.BlockSpec((1,) + a.shape[1:], lambda b, p, pt: (b,) + (0,) * (a.ndim - 1))


def _scratch_init(m_s, l_s, acc_s):
    m_s[...] = jnp.full(m_s.shape, NEG, F32)
    l_s[...] = jnp.zeros(l_s.shape, F32)
    acc_s[...] = jnp.zeros(acc_s.shape, F32)


def _scratch_update(m_s, l_s, acc_s, s, pv):
    m_new, l_new, acc_new = _online((m_s[...], l_s[...], acc_s[...]), s, pv)
    m_s[...], l_s[...], acc_s[...] = m_new, l_new, acc_new


def _paged_pv(vts):
    def pv(p):
        return sum(_dot_nt(p[:, PAGE_SIZE * j:PAGE_SIZE * (j + 1)], vts[j][...].astype(BF16)) for j in range(len(vts)))
    return pv


def _fox_s_kernel(pt_ref, q_ref, lrows_ref, kn_ref, vn_ref, lfn_ref, ck_ref, cv_ref, clf_ref, o_ref,
                  m_s, l_s, acc_s, carry_s, kbuf, vbuf, lbuf, sem, *, pg, n_tok, layer, page_of):
    kps, vps, lps = _paged_fetch(pt_ref, (ck_ref, cv_ref, clf_ref), (kbuf, vbuf, lbuf), sem, layer, pg, page_of)
    p = pl.program_id(1)
    rows = q_ref.shape[1]
    q = (q_ref[0] * FOX_SCALE).astype(BF16)
    lane = _iota((rows, PAGE_SIZE), 1)
    t_row = _iota((rows, PAGE_SIZE), 0) // N_HEADS
    sp = _iota((PAGE_SIZE, 2 * PAGE_SIZE), 0)
    cc = _iota((PAGE_SIZE, 2 * PAGE_SIZE), 1)
    msk = jnp.where(((cc < PAGE_SIZE) & (sp >= cc)) | ((cc >= PAGE_SIZE) & (sp > cc - PAGE_SIZE)), 1.0, 0.0).astype(BF16)
    rowc = jnp.sum(jnp.where(lane > t_row, lrows_ref[0], 0.0), axis=-1, keepdims=True)

    def logits(kts, lfts, carry):
        out = []
        for kt, lft in zip(kts, lfts):
            sfx = sum(_dot(part, msk) for part in _split3(lft))
            bias8 = sfx[:, PAGE_SIZE:] + carry
            carry = carry + sfx[:, 0:1]
            out.append(_dot(q, kt.astype(BF16)) + (_rows([bias8] * n_tok) - rowc))
        return out, carry

    @pl.when(p == 0)
    def _():
        _scratch_init(m_s, l_s, acc_s)
        (s,), carry = logits([kn_ref[0]], [lfn_ref[0]], jnp.zeros(carry_s.shape, F32))
        carry_s[...] = carry
        _scratch_update(m_s, l_s, acc_s, jnp.where(lane <= t_row, s, NEG), _paged_pv([vn_ref.at[0]]))

    ss, carry = logits([r[...] for r in kps], [r[...] for r in lps], carry_s[...])
    carry_s[...] = carry
    _scratch_update(m_s, l_s, acc_s, jnp.concatenate(ss, axis=1), _paged_pv(vps))

    @pl.when(p == pl.num_programs(1) - 1)
    def _():
        o_ref[0] = acc_s[...] / l_s[...]


def _fox_sample(pt, q, lrows, knt, vnt, lfnt, ckt, cvt, clft, layer, pg):
    bs, npages = pt.shape
    rows = q.shape[1]
    rev = lambda p, j: npages - 1 - (p * pg + j)
    grid_spec = pltpu.PrefetchScalarGridSpec(
        num_scalar_prefetch=1, grid=(bs, npages // pg),
        in_specs=[_per_b(q), _per_b(lrows), _per_b(knt), _per_b(vnt), _per_b(lfnt), _HBM, _HBM, _HBM],
        out_specs=pl.BlockSpec((1, rows, LANES), lambda b, p, pt: (b, 0, 0)),
        scratch_shapes=[pltpu.VMEM((rows, 1), F32), pltpu.VMEM((rows, 1), F32), pltpu.VMEM((rows, LANES), F32),
                        pltpu.VMEM((FOX_HEADS, 1), F32)] + _page_scratch((ckt, cvt, clft), pg))
    return pl.pallas_call(
        functools.partial(_fox_s_kernel, pg=pg, n_tok=rows // N_HEADS, layer=layer, page_of=rev),
        grid_spec=grid_spec, out_shape=jax.ShapeDtypeStruct((bs, rows, LANES), F32),
        compiler_params=_cparams(2), name='fox_sample',
    )(pt, q, lrows, knt, vnt, lfnt, ckt, cvt, clft)


def _mla_s_kernel(pt_ref, ql_ref, qr_ref, cn_ref, rn_ref, cc_ref, cr_ref, o_ref, m_s, l_s, acc_s, cbuf, rbuf, sem,
                  *, pg, layer, page_of):
    cps, rps = _paged_fetch(pt_ref, (cc_ref, cr_ref), (cbuf, rbuf), sem, layer, pg, page_of)
    p = pl.program_id(1)
    rows = ql_ref.shape[1]
    ql = ql_ref[0].astype(BF16)
    qr = qr_ref[0].astype(BF16)
    lane = _iota((rows, PAGE_SIZE), 1)
    t_row = _iota((rows, PAGE_SIZE), 0) // N_HEADS

    def logits(c, rt):
        return _dot_nt(ql, c.astype(BF16)) + _dot(qr, rt.astype(BF16))

    def pv_of(cs):
        def pv(pr):
            return sum(_dot(pr[:, PAGE_SIZE * j:PAGE_SIZE * (j + 1)], cs[j][...].astype(BF16)) for j in range(len(cs)))
        return pv

    @pl.when(p == 0)
    def _():
        _scratch_init(m_s, l_s, acc_s)
        s = logits(cn_ref[0], rn_ref[0])
        _scratch_update(m_s, l_s, acc_s, jnp.where(lane <= t_row, s, NEG), pv_of([cn_ref.at[0]]))

    s = jnp.concatenate([logits(cps[j][...], rps[j][...]) for j in range(pg)], axis=1)
    _scratch_update(m_s, l_s, acc_s, s, pv_of(cps))

    @pl.when(p == pl.num_programs(1) - 1)
    def _():
        o_ref[0] = acc_s[...] / l_s[...]


def _mla_sample(pt, ql, qr, cn, rnt, cckv, ckrt, layer, pg):
    bs, npages = pt.shape
    rows = ql.shape[1]
    fwd = lambda p, j: p * pg + j
    grid_spec = pltpu.PrefetchScalarGridSpec(
        num_scalar_prefetch=1, grid=(bs, npages // pg),
        in_specs=[_per_b(ql), _per_b(qr), _per_b(cn), _per_b(rnt), _HBM, _HBM],
        out_specs=pl.BlockSpec((1, rows, MLA_KV_LORA), lambda b, p, pt: (b, 0, 0)),
        scratch_shapes=[pltpu.VMEM((rows, 1), F32), pltpu.VMEM((rows, 1), F32), pltpu.VMEM((rows, MLA_KV_LORA), F32)]
        + _page_scratch((cckv, ckrt), pg))
    return pl.pallas_call(
        functools.partial(_mla_s_kernel, pg=pg, layer=layer, page_of=fwd),
        grid_spec=grid_spec, out_shape=jax.ShapeDtypeStruct((bs, rows, MLA_KV_LORA), F32),
        compiler_params=_cparams(2), name='mla_sample',
    )(pt, ql, qr, cn, rnt, cckv, ckrt)


def _mla_up_kernel(o_ref, wv_ref, out_ref):
    full = _dot(o_ref[0].astype(BF16), wv_ref[...])
    head = _iota(full.shape, 0) % MLA_HEADS
    keep = (_iota(full.shape, 1) // MLA_V) == head
    out_ref[0] = jnp.where(keep, full, 0.0)


def _mla_up(olat, wv_all):
    bs, rows, _ = olat.shape
    return pl.pallas_call(
        _mla_up_kernel, grid=(bs,),
        in_specs=[pl.BlockSpec((1, rows, MLA_KV_LORA), lambda b: (b, 0, 0)), pl.BlockSpec(wv_all.shape, lambda b: (0, 0))],
        out_specs=pl.BlockSpec((1, rows, MLA_W), lambda b: (b, 0, 0)),
        out_shape=jax.ShapeDtypeStruct((bs, rows, MLA_W), F32), compiler_params=_cparams(1), name='mla_up_sample',
    )(olat, wv_all)


def _dsa_idx_s_kernel(pt_ref, iq_ref, iw_ref, ikn_ref, ci_ref, bias_ref, keys_ref, ibuf, sem,
                      *, pg, topk, n_tok, layer, page_of):
    (ips,) = _paged_fetch(pt_ref, (ci_ref,), (ibuf,), sem, layer, pg, page_of)
    p = pl.program_id(1)
    rows = iq_ref.shape[1]
    steps = keys_ref.shape[0] - 1
    trows = keys_ref.shape[1]
    iq = iq_ref[0].astype(BF16)
    iw = iw_ref[0]
    tok = _iota((trows, PAGE_SIZE), 0)
    lane = _iota((trows, PAGE_SIZE), 1)
    pad_rows = jnp.zeros((trows - n_tok, PAGE_SIZE), F32)

    def score_keys(ikt):
        w = jnp.maximum(_dot(iq, ikt.astype(BF16)), 0.0) * iw
        tot = jnp.sum(w.reshape(n_tok, IDX_HEADS, PAGE_SIZE), axis=1)
        return jnp.where(tok < n_tok, _sort_key(jnp.concatenate([tot, pad_rows], axis=0)), INT_MIN)

    @pl.when(p == 0)
    def _():
        keys_ref[steps] = jnp.full(keys_ref.shape[1:], INT_MIN, I32)
        keys_ref[steps, :, 0:PAGE_SIZE] = jnp.where(lane <= tok, score_keys(ikn_ref[0]), INT_MIN)

    for j in range(pg):
        keys_ref[p, :, PAGE_SIZE * j:PAGE_SIZE * (j + 1)] = score_keys(ips[j][...])

    def write(j, bias):
        bias_ref[0, j] = _rows([jnp.broadcast_to(bias[t:t + 1, :], (IDX_HEADS, bias.shape[1])) for t in range(n_tok)])

    @pl.when(p == steps - 1)
    def _():
        _select_bias(keys_ref, write, topk, steps + 1)


def _dsa_idx_sample(pt, iq, iw, iknt, cidxt, layer, pg, topk):
    bs, npages = pt.shape
    rows = iq.shape[1]
    fwd = lambda p, j: p * pg + j
    steps = npages // pg
    grid_spec = pltpu.PrefetchScalarGridSpec(
        num_scalar_prefetch=1, grid=(bs, steps),
        in_specs=[_per_b(iq), _per_b(iw), _per_b(iknt), _HBM],
        out_specs=pl.BlockSpec((1, steps + 1, rows, pg * PAGE_SIZE), lambda b, p, pt: (b, 0, 0, 0)),
        scratch_shapes=[pltpu.VMEM((steps + 1, SUBLANES, pg * PAGE_SIZE), I32)] + _page_scratch((cidxt,), pg))
    assert rows // IDX_HEADS <= SUBLANES
    return pl.pallas_call(
        functools.partial(_dsa_idx_s_kernel, pg=pg, topk=topk, n_tok=rows // IDX_HEADS, layer=layer, page_of=fwd),
        grid_spec=grid_spec, out_shape=jax.ShapeDtypeStruct((bs, steps + 1, rows, pg * PAGE_SIZE), F32),
        compiler_params=_cparams(2), name='dsa_index_sample',
    )(pt, iq, iw, iknt, cidxt)


def _dsa_s_kernel(pt_ref, q_ref, kn_ref, vn_ref, bn_ref, bp_ref, ck_ref, cv_ref, o_ref, m_s, l_s, acc_s, kbuf, vbuf, sem,
                  *, pg, layer, page_of):
    kps, vps = _paged_fetch(pt_ref, (ck_ref, cv_ref), (kbuf, vbuf), sem, layer, pg, page_of)
    p = pl.program_id(1)
    q = q_ref[0].astype(BF16)

    @pl.when(p == 0)
    def _():
        _scratch_init(m_s, l_s, acc_s)
        _scratch_update(m_s, l_s, acc_s, _dot(q, kn_ref[0].astype(BF16)) + bn_ref[0, :, 0:PAGE_SIZE],
                        _paged_pv([vn_ref.at[0]]))

    s = jnp.concatenate([_dot(q, kps[j][...].astype(BF16)) for j in range(pg)], axis=1) + bp_ref[0]
    _scratch_update(m_s, l_s, acc_s, s, _paged_pv(vps))

    @pl.when(p == pl.num_programs(1) - 1)
    def _():
        o_ref[0] = acc_s[...] / l_s[...]


def _dsa_sample(pt, q, knt, vnt, bias, ckt, cvt, layer, pg):
    bs, npages = pt.shape
    rows = q.shape[1]
    fwd = lambda p, j: p * pg + j
    grid_spec = pltpu.PrefetchScalarGridSpec(
        num_scalar_prefetch=1, grid=(bs, npages // pg),
        in_specs=[_per_b(q), _per_b(knt), _per_b(vnt),
                  pl.BlockSpec((None, 1, rows, pg * PAGE_SIZE), lambda b, p, pt: (b, npages // pg, 0, 0)),
                  pl.BlockSpec((None, 1, rows, pg * PAGE_SIZE), lambda b, p, pt: (b, p, 0, 0)), _HBM, _HBM],
        out_specs=pl.BlockSpec((1, rows, LANES), lambda b, p, pt: (b, 0, 0)),
        scratch_shapes=[pltpu.VMEM((rows, 1), F32), pltpu.VMEM((rows, 1), F32), pltpu.VMEM((rows, LANES), F32)]
        + _page_scratch((ckt, cvt), pg))
    return pl.pallas_call(
        functools.partial(_dsa_s_kernel, pg=pg, layer=layer, page_of=fwd),
        grid_spec=grid_spec, out_shape=jax.ShapeDtypeStruct((bs, rows, LANES), F32),
        compiler_params=_cparams(2), name='dsa_sample',
    )(pt, q, knt, vnt, bias, bias, ckt, cvt)


def _pad_cols(w):
    zeros = lambda n: jnp.zeros(w.shape[:2] + (n,), w.dtype)
    parts = []
    for name, pad, slotting in _GROUPS:
        if name == 'pad':
            parts.append(zeros(pad))
            continue
        src, real = _SRC[name], _WID[name]
        if slotting is None:
            parts.append(w[:, :, src:src + real])
            if pad > real:
                parts.append(zeros(pad - real))
            continue
        for h in range(N_HEADS):
            head = w[:, :, src + HEAD_DIM * h:src + HEAD_DIM * (h + 1)]
            first = slotting == 'lo' or h // GROUP == 0
            parts.extend([head, zeros(HEAD_DIM)] if first else [zeros(HEAD_DIM), head])
    return jnp.concatenate(parts, axis=-1)


def _rot_tables(pos, half, width, reps):
    inv_freq = jnp.exp(-math.log(ROPE_THETA) * jnp.arange(half, dtype=F32) / half)
    ang = pos.astype(F32)[:, None] * inv_freq[None, :]
    cos, sin = jnp.cos(ang), jnp.sin(ang)
    n = pos.shape[0]
    cos_h = jnp.concatenate([cos, cos, jnp.ones((n, width - 2 * half), F32)], axis=-1)
    sin_h = jnp.concatenate([-sin, sin, jnp.zeros((n, width - 2 * half), F32)], axis=-1)
    return jnp.tile(cos_h, (1, reps)), jnp.tile(sin_h, (1, reps))


def _new_page_t(a):
    return jnp.pad(jnp.swapaxes(a, 1, 2), ((0, 0), (0, 0), (0, PAGE_SIZE - a.shape[1])))


def kernel(x_prompt, x_sample, cache_fox_k, cache_fox_v, cache_fox_logf, cache_mla_ckv, cache_mla_krope,
           cache_dsa_k, cache_dsa_v, cache_dsa_kidx, cache_mem_k, cache_mem_v, page_table, mem_prompt,
           w_in, b_forget, mla_q_norm, mla_kv_norm, w_mla_q_up, w_mla_kv_up, w_mem_kv, w_branch, w_out,
           ln_gain, ln_bias):
    b, t, d = x_prompt.shape
    bs, ts, _ = x_sample.shape
    depth = w_in.shape[0]
    mem_len = mem_prompt.shape[1]
    npages = page_table.shape[1]
    past_len = npages * PAGE_SIZE
    pool = cache_fox_k.shape[0]
    rows = ts * N_HEADS
    alpha = (2 * depth) ** 0.25
    assert FOX_HEADS == MLA_HEADS == DSA_HEADS == IDX_HEADS == N_HEADS and ts <= PAGE_SIZE

    tq = min(128, t)
    tq_dsa = min(256, t)
    tk = min(512, t)
    tk_mla = min(256, t)
    tr = min(256, t)
    pg = math.gcd(npages, 32)
    n_s = bs * ts
    topk_s = min(TOPK_MAX, (past_len + ts) // 4)

    w_pad = _pad_cols(w_in).astype(BF16)
    bf_pad = jnp.pad(b_forget, ((0, 0), (0, LANES - FOX_HEADS)))[:, None, :]
    qn_pad = jnp.pad(mla_q_norm, ((0, 0), (0, 512 - MLA_Q_LORA)))[:, None, :]
    kvn = mla_kv_norm[:, None, :]
    wq = w_mla_q_up.reshape(depth, MLA_Q_LORA, MLA_HEADS, MLA_NOPE + MLA_ROPE)
    wq_pad = lambda a: jnp.pad(a, ((0, 0), (0, 512 - MLA_Q_LORA), (0, 0), (0, LANES - a.shape[-1]))
                               ).reshape(depth, 512, MLA_HEADS * LANES).astype(BF16)
    wq_nope, wq_rope = wq_pad(wq[..., :MLA_NOPE]), wq_pad(wq[..., MLA_NOPE:])
    wkt = jnp.pad(jnp.transpose(w_mla_kv_up[..., :MLA_NOPE], (0, 2, 3, 1)), ((0, 0), (0, 0), (0, LANES - MLA_NOPE), (0, 0))
                  ).reshape(depth, MLA_HEADS * LANES, MLA_KV_LORA).astype(BF16)
    wv = jnp.transpose(w_mla_kv_up[..., MLA_NOPE:], (0, 2, 1, 3))
    wv_slot = jnp.stack([jnp.pad(wv[:, h], ((0, 0), (0, 0), ((h % 2) * MLA_V, (1 - h % 2) * MLA_V)))
                         for h in range(MLA_HEADS)], axis=1).astype(BF16)
    wv_all = jnp.transpose(wv, (0, 2, 1, 3)).reshape(depth, MLA_KV_LORA, MLA_W).astype(BF16)
    w_mem = w_mem_kv.astype(BF16)
    wb, wo = w_branch.astype(BF16), w_out.astype(BF16)
    ln_g, ln_b = ln_gain[:, None, :], ln_bias[:, None, :]

    pos_p = jnp.arange(t, dtype=jnp.int32)
    pos_s = jnp.tile(past_len + jnp.arange(ts, dtype=jnp.int32), bs)
    tabs_p = _rot_tables(pos_p, ROT_DIM // 2, HEAD_DIM, 2) + _rot_tables(pos_p, MLA_ROPE // 2, MLA_ROPE, 4)
    tabs_s = _rot_tables(pos_s, ROT_DIM // 2, HEAD_DIM, 2) + _rot_tables(pos_s, MLA_ROPE // 2, MLA_ROPE, 4)

    kv_t = lambda c: jnp.transpose(c, (0, 1, 3, 4, 2)).reshape(pool, depth, FOX_KV_HEADS * HEAD_DIM, PAGE_SIZE)
    c_fkt, c_fvt, c_dkt, c_dvt = kv_t(cache_fox_k), kv_t(cache_fox_v), kv_t(cache_dsa_k), kv_t(cache_dsa_v)
    c_lft = jnp.swapaxes(cache_fox_logf, 2, 3)
    c_krt = jnp.swapaxes(cache_mla_krope, 2, 3)
    c_ixt = jnp.swapaxes(cache_dsa_kidx, 2, 3)
    c_mk = cache_mem_k.reshape(bs, depth, mem_len, MEM_W)
    c_mv = cache_mem_v.reshape(bs, depth, mem_len, MEM_W)

    xp = x_prompt.reshape(b * t, d)
    xs = x_sample.reshape(n_s, d)
    mem2 = mem_prompt.reshape(b * mem_len, d)
    P = {k: [] for k in ('fox_k', 'fox_v', 'logf', 'ckv', 'kr', 'dsa_k', 'dsa_v', 'ik', 'mem_k', 'mem_v')}
    S = {k: [] for k in ('fox_k', 'fox_v', 'logf', 'ckv', 'kr', 'dsa_k', 'dsa_v', 'ik')}

    def col(h2, name, width=None):
        off = _LAYOUT[name][0]
        return h2[:, off:off + (width or _LAYOUT[name][1])]

    for l in range(depth):
        params = (bf_pad[l], qn_pad[l], kvn[l], wq_nope[l], wq_rope[l], wkt[l])

        h2 = _matmul(xp, w_pad[l], min(1024, b * t), PROJ_TN)
        pr = _prep(h2, tabs_p, params, tr)
        h3 = h2.reshape(b, t, EP)
        r3 = lambda a: a.reshape(b, t, a.shape[-1])
        c, ct = _cumsum(r3(pr['logf']))
        fox_o = _fox_prompt(h3, c, ct, tq, tk)
        mla_o = _mla_prompt(r3(pr['qlat']), r3(pr['qrope']), r3(pr['ckv']), r3(pr['kr']), wv_slot[l], tq, tk_mla)
        dsa_o = _dsa_prompt(r3(pr['iq']), r3(pr['iw']), r3(pr['ik']), r3(pr['dq']), r3(pr['dk']), h3, tq_dsa, tk)
        mkv = _matmul(mem2, w_mem[l], min(1024, b * mem_len), 2 * MEM_W).reshape(b, mem_len, 2 * MEM_W)
        tqm = min(256, t)
        mem_o = _mem_attention(
            h3, pl.BlockSpec((1, tqm, MEM_W), lambda bi, i: (bi, i, _blk('mem_q'))),
            mkv, pl.BlockSpec((1, mem_len, MEM_W), lambda bi, i: (bi, 0, 0)),
            mkv, pl.BlockSpec((1, mem_len, MEM_W), lambda bi, i: (bi, 0, 1)),
            (b, t // tqm), (b, t, MEM_W), pl.BlockSpec((1, tqm, MEM_W), lambda bi, i: (bi, i, 0)))
        xp_new = _finish(xp, h2, fox_o.reshape(b * t, -1), mla_o.reshape(b * t, -1), dsa_o.reshape(b * t, -1),
                         mem_o.reshape(b * t, -1), wb[l], wo[l], ln_g[l], ln_b[l], tr, alpha)
        P['fox_k'].append(col(h2, 'fox_k')); P['fox_v'].append(col(h2, 'fox_v')); P['logf'].append(pr['logf'])
        P['ckv'].append(pr['ckv']); P['kr'].append(pr['kr'][:, :MLA_ROPE]); P['dsa_k'].append(pr['dk'])
        P['dsa_v'].append(col(h2, 'dsa_v')); P['ik'].append(pr['ik'][:, :IDX_DIM])
        P['mem_k'].append(mkv[..., :MEM_W]); P['mem_v'].append(mkv[..., MEM_W:])
        xp = xp_new

        g2 = _matmul(xs, w_pad[l], n_s, PROJ_TN)
        sr = _prep(g2, tabs_s, params, n_s)
        rows_of = lambda a: a.reshape(bs, rows, a.shape[-1] // N_HEADS)
        new3 = lambda a: a.reshape(bs, ts, a.shape[-1])
        fold = lambda o: jnp.where(((jnp.arange(rows) % N_HEADS) // GROUP == 0)[None, :, None],
                                   o[..., :HEAD_DIM], o[..., HEAD_DIM:]).reshape(n_s, N_HEADS * HEAD_DIM)
        lf_new = new3(sr['logf'])
        lrows = jnp.broadcast_to(jnp.transpose(lf_new, (0, 2, 1))[:, None], (bs, ts, FOX_HEADS, ts)).reshape(bs, rows, ts)
        lrows = jnp.pad(lrows, ((0, 0), (0, 0), (0, LANES - ts)))
        fox_s = fold(_fox_sample(page_table, rows_of(col(g2, 'fox_q')), lrows,
                                 _new_page_t(new3(col(g2, 'fox_k'))), _new_page_t(new3(col(g2, 'fox_v'))),
                                 _new_page_t(lf_new), c_fkt, c_fvt, c_lft, l, pg))
        olat = _mla_sample(page_table, rows_of(sr['qlat']), rows_of(sr['qrope'])[..., :MLA_ROPE],
                           _new_page_t(new3(sr['ckv'])).swapaxes(1, 2), _new_page_t(new3(sr['kr'][:, :MLA_ROPE])),
                           cache_mla_ckv, c_krt, l, pg)
        mla_s = _mla_up(olat, wv_all[l]).reshape(bs, rows, MLA_HEADS, MLA_V).sum(axis=2).reshape(n_s, MLA_W)
        bias = _dsa_idx_sample(page_table, rows_of(sr['iq'])[..., :IDX_DIM], sr['iw'].reshape(bs, rows, 1),
                               _new_page_t(new3(sr['ik'][:, :IDX_DIM])), c_ixt, l, pg, topk_s)
        dsa_s = fold(_dsa_sample(page_table, rows_of(sr['dq']), _new_page_t(new3(sr['dk'])),
                                 _new_page_t(new3(col(g2, 'dsa_v'))), bias, c_dkt, c_dvt, l, pg))
        mem_s = _mem_attention(
            new3(col(g2, 'mem_q')), pl.BlockSpec((1, ts, MEM_W), lambda bi: (bi, 0, 0)),
            c_mk, pl.BlockSpec((None, None, mem_len, MEM_W), lambda bi, l=l: (bi, l, 0, 0)),
            c_mv, pl.BlockSpec((None, None, mem_len, MEM_W), lambda bi, l=l: (bi, l, 0, 0)),
            (bs,), (bs, ts, MEM_W), pl.BlockSpec((1, ts, MEM_W), lambda bi: (bi, 0, 0))).reshape(n_s, MEM_W)
        xs_new = _finish(xs, g2, fox_s, mla_s, dsa_s, mem_s, wb[l], wo[l], ln_g[l], ln_b[l], n_s, alpha)
        S['fox_k'].append(col(g2, 'fox_k')); S['fox_v'].append(col(g2, 'fox_v')); S['logf'].append(sr['logf'])
        S['ckv'].append(sr['ckv']); S['kr'].append(sr['kr'][:, :MLA_ROPE]); S['dsa_k'].append(sr['dk'])
        S['dsa_v'].append(col(g2, 'dsa_v')); S['ik'].append(sr['ik'][:, :IDX_DIM])
        xs = xs_new

    def stack(lst, lead, tail):
        return jnp.stack([a.reshape(lead + tail) for a in lst], axis=1)

    kv2 = (FOX_KV_HEADS, HEAD_DIM)
    pl_, sl_ = (b, t), (bs, ts)
    return (xp.reshape(b, t, d), xs.reshape(bs, ts, d),
            stack(P['fox_k'], pl_, kv2), stack(P['fox_v'], pl_, kv2), stack(P['logf'], pl_, (FOX_HEADS,)),
            stack(P['ckv'], pl_, (MLA_KV_LORA,)), stack(P['kr'], pl_, (MLA_ROPE,)),
            stack(P['dsa_k'], pl_, kv2), stack(P['dsa_v'], pl_, kv2), stack(P['ik'], pl_, (IDX_DIM,)),
            stack(P['mem_k'], (b, mem_len), (MEM_HEADS, HEAD_DIM)), stack(P['mem_v'], (b, mem_len), (MEM_HEADS, HEAD_DIM)),
            stack(S['fox_k'], sl_, kv2), stack(S['fox_v'], sl_, kv2), stack(S['logf'], sl_, (FOX_HEADS,)),
            stack(S['ckv'], sl_, (MLA_KV_LORA,)), stack(S['kr'], sl_, (MLA_ROPE,)),
            stack(S['dsa_k'], sl_, kv2), stack(S['dsa_v'], sl_, kv2), stack(S['ik'], sl_, (IDX_DIM,)))
```

```python
import functools
import math

import numpy as np
import jax
import jax.numpy as jnp
from jax import lax
from jax.experimental import pallas as pl
from jax.experimental.pallas import tpu as pltpu

F32 = jnp.float32
BF16 = jnp.bfloat16
I32 = jnp.int32

HEAD_DIM = 64
ROT_DIM = HEAD_DIM // 4
ROPE_THETA = 500000.0
FOX_HEADS = 8
FOX_KV_HEADS = 2
MLA_HEADS = 8
MLA_NOPE = 64
MLA_ROPE = 32
MLA_V = 64
MLA_Q_LORA = 384
MLA_KV_LORA = 256
DSA_HEADS = 8
DSA_KV_HEADS = 2
IDX_HEADS = 8
IDX_DIM = 64
TOPK_MAX = 256
MEM_HEADS = 4
N_BRANCH = 4
PAGE_SIZE = 128
N_HEADS = 8
GROUP = N_HEADS // FOX_KV_HEADS
FOX_W = FOX_HEADS * HEAD_DIM
MLA_W = MLA_HEADS * MLA_V
DSA_W = DSA_HEADS * HEAD_DIM
MEM_W = MEM_HEADS * HEAD_DIM
BRANCH_WIDTHS = (FOX_W, MLA_W, DSA_W, MEM_W)
FOX_SCALE = HEAD_DIM ** -0.5
MLA_SCALE = (MLA_NOPE + MLA_ROPE) ** -0.5
DSA_SCALE = HEAD_DIM ** -0.5
MEM_SCALE = HEAD_DIM ** -0.5
IDX_SCALE = IDX_DIM ** -0.5
IDX_W_SCALE = IDX_HEADS ** -0.5
LN_EPS = 1e-5
RMS_EPS = 1e-6

LANES = 128
SUBLANES = 8
VMEM_LIMIT_BYTES = 48 * 2 ** 20

NEG = -1e30
INT_MIN = -2 ** 31

_IN_NAMES = ('fox_q', 'fox_k', 'fox_v', 'fox_f', 'fox_z', 'mla_qa', 'mla_kva', 'mla_kr', 'mla_z',
             'dsa_q', 'dsa_k', 'dsa_v', 'idx_q', 'idx_k', 'idx_w', 'dsa_z', 'mem_q', 'mem_z', 'gates')
_IN_SPLITS = (512, 128, 128, 8, 512, 384, 256, 32, 512, 512, 128, 128, 512, 64, 8, 512, 256, 256, 4096)
_SRC = dict(zip(_IN_NAMES, np.cumsum((0,) + _IN_SPLITS[:-1]).tolist()))
_WID = dict(zip(_IN_NAMES, _IN_SPLITS))
_GROUPS = (
    ('gates', 4096, None),
    ('fox_q', 1024, 'kv'), ('dsa_q', 1024, 'kv'), ('idx_q', 1024, 'lo'),
    ('fox_z', 512, None), ('mla_z', 512, None), ('dsa_z', 512, None), ('mla_qa', 512, None),
    ('mla_kva', 256, None), ('mem_q', 256, None), ('mem_z', 256, None),
    ('fox_k', 128, None), ('fox_v', 128, None), ('dsa_k', 128, None), ('dsa_v', 128, None),
    ('idx_k', 128, None), ('mla_kr', 128, None), ('fox_f', 128, None), ('idx_w', 128, None),
    ('pad', 256, None),
)
PROJ_TN = 1024


def _layout():
    off, out = 0, {}
    for name, pad, _ in _GROUPS:
        assert off % pad == 0
        out[name] = (off, pad)
        off += pad
    return out, off


_LAYOUT, EP = _layout()


def _blk(name):
    off, pad = _LAYOUT[name]
    return off // pad


def _cparams(n_grid):
    return pltpu.CompilerParams(dimension_semantics=('arbitrary',) * n_grid, vmem_limit_bytes=VMEM_LIMIT_BYTES)


def _dot(a, b):
    return jnp.dot(a, b, preferred_element_type=F32)


def _dot_nt(a, b):
    return lax.dot_general(a, b, (((1,), (1,)), ((), ())), preferred_element_type=F32)


def _dot_tn(a, b):
    return lax.dot_general(a, b, (((0,), (0,)), ((), ())), preferred_element_type=F32)


def _split3(x):
    hi = x.astype(BF16)
    r = x - hi.astype(F32)
    mid = r.astype(BF16)
    lo = (r - mid.astype(F32)).astype(BF16)
    return hi, mid, lo


def _iota(shape, dim):
    return lax.broadcasted_iota(I32, shape, dim)


def _rows(blocks):
    return jnp.concatenate(blocks, axis=0)


def _mm_kernel(x_ref, w_ref, o_ref, xb_ref):
    @pl.when(pl.program_id(1) == 0)
    def _():
        xb_ref[...] = x_ref[...].astype(BF16)

    o_ref[...] = _dot(xb_ref[...], w_ref[...])


def _matmul(x, w, tm, tn):
    m, k = x.shape
    n = w.shape[1]
    assert m % tm == 0 and n % tn == 0 and w.dtype == BF16
    return pl.pallas_call(
        _mm_kernel,
        grid=(m // tm, n // tn),
        in_specs=[pl.BlockSpec((tm, k), lambda i, j: (i, 0)), pl.BlockSpec((k, tn), lambda i, j: (0, j))],
        out_specs=pl.BlockSpec((tm, tn), lambda i, j: (i, j)),
        out_shape=jax.ShapeDtypeStruct((m, n), F32),
        scratch_shapes=[pltpu.VMEM((tm, k), BF16)],
        compiler_params=_cparams(2),
        name='proj_matmul',
    )(x, w)


def _rope(x, cos, sin, half, width):
    w = x.shape[-1]
    d = _iota(x.shape, 1) % width
    up = pltpu.roll(x, w - half, 1)
    dn = pltpu.roll(x, half, 1)
    partner = jnp.where(d < half, up, dn)
    return jnp.where(d < 2 * half, x * cos + partner * sin, x)


def _tile_lanes(x, n):
    return jnp.concatenate([x] * n, axis=-1) if n > 1 else x


def _prep_kernel(qa_ref, kva_ref, dq_ref, dk_ref, iq_ref, ik_ref, kr_ref, ff_ref, iw_ref,
                 ca_ref, sa_ref, cb_ref, sb_ref, bf_ref, qn_ref, kvn_ref, wqn_ref, wqr_ref, wkt_ref,
                 logf_ref, ckv_ref, kro_ref, dqo_ref, dko_ref, iqo_ref, iko_ref, iwo_ref, qlat_ref, qrope_ref):
    ca, sa, cb, sb = ca_ref[...], sa_ref[...], cb_ref[...], sb_ref[...]
    half_a, half_b = ROT_DIM // 2, MLA_ROPE // 2
    logf_ref[...] = jax.nn.log_sigmoid(ff_ref[...] + bf_ref[...])[:, :FOX_HEADS]
    kva = kva_ref[...]
    ckv_ref[...] = kva * lax.rsqrt(jnp.mean(kva * kva, axis=-1, keepdims=True) + RMS_EPS) * kvn_ref[...]
    kro_ref[...] = _rope(kr_ref[...], cb, sb, half_b, MLA_ROPE)
    dqo_ref[...] = (_rope(dq_ref[...], _tile_lanes(ca, N_HEADS), _tile_lanes(sa, N_HEADS), half_a, HEAD_DIM)
                    * DSA_SCALE).astype(BF16)
    dko_ref[...] = _rope(dk_ref[...], ca, sa, half_a, HEAD_DIM)
    iqo_ref[...] = (_rope(iq_ref[...], _tile_lanes(ca, N_HEADS), _tile_lanes(sa, N_HEADS), half_a, IDX_DIM)
                    * IDX_SCALE).astype(BF16)
    iko_ref[...] = _rope(ik_ref[...], ca, sa, half_a, IDX_DIM)
    iwo_ref[...] = (iw_ref[...] * IDX_W_SCALE)[:, :IDX_HEADS]
    qa = qa_ref[...]
    ms = jnp.sum(qa * qa, axis=-1, keepdims=True) * (1.0 / MLA_Q_LORA)
    qan = (qa * lax.rsqrt(ms + RMS_EPS) * qn_ref[...]).astype(BF16)
    q_nope = _dot(qan, wqn_ref[...]).astype(BF16)
    q_rope = _dot(qan, wqr_ref[...])
    qrope_ref[...] = (_rope(q_rope, _tile_lanes(cb, N_HEADS), _tile_lanes(sb, N_HEADS), half_b, MLA_ROPE)
                      * MLA_SCALE).astype(BF16)
    for h in range(MLA_HEADS):
        qlat_ref[:, MLA_KV_LORA * h:MLA_KV_LORA * (h + 1)] = (_dot(
            q_nope[:, LANES * h:LANES * (h + 1)], wkt_ref[LANES * h:LANES * (h + 1), :]) * MLA_SCALE).astype(BF16)


def _prep(h2, tabs, params, tr):
    n = h2.shape[0]
    nt = tabs[0].shape[0] // tr
    row = lambda name: pl.BlockSpec((tr, _LAYOUT[name][1]), lambda i, name=name: (i, _blk(name)))
    tab = pl.BlockSpec((tr, LANES), lambda i: (i % nt, 0))
    full = lambda a: pl.BlockSpec(a.shape, lambda i: (0,) * a.ndim)
    names = ('mla_qa', 'mla_kva', 'dsa_q', 'dsa_k', 'idx_q', 'idx_k', 'mla_kr', 'fox_f', 'idx_w')
    in_specs = [row(nm) for nm in names] + [tab] * 4 + [full(p) for p in params]
    outs = [('logf', FOX_HEADS), ('ckv', MLA_KV_LORA), ('kr', LANES), ('dq', N_HEADS * LANES), ('dk', LANES),
            ('iq', N_HEADS * LANES), ('ik', LANES), ('iw', IDX_HEADS), ('qlat', MLA_HEADS * MLA_KV_LORA),
            ('qrope', N_HEADS * LANES)]
    res = pl.pallas_call(
        _prep_kernel,
        grid=(n // tr,),
        in_specs=in_specs,
        out_specs=[pl.BlockSpec((tr, w), lambda i: (i, 0)) for _, w in outs],
        out_shape=[jax.ShapeDtypeStruct((n, w), BF16 if k in ('dq', 'iq', 'qlat', 'qrope') else F32) for k, w in outs],
        compiler_params=_cparams(1),
        name='prep_rows',
    )(*([h2] * len(names)), *tabs, *params)
    return dict(zip([k for k, _ in outs], res))


def _cumsum_kernel(lf_ref, c_ref, ct_ref, *, blk):
    t = lf_ref.shape[1]
    r, c = _iota((blk, blk), 0), _iota((blk, blk), 1)
    tri = jnp.where(c <= r, 1.0, 0.0).astype(BF16)
    trit = jnp.where(r <= c, 1.0, 0.0).astype(BF16)
    carry = jnp.zeros((1, FOX_HEADS), F32)
    carry_t = jnp.zeros((FOX_HEADS, 1), F32)
    for i in range(t // blk):
        parts = _split3(lf_ref[0, blk * i:blk * (i + 1), :])
        cb = sum(_dot(tri, p) for p in parts) + carry
        cbt = sum(_dot_tn(p, trit) for p in parts) + carry_t
        c_ref[0, blk * i:blk * (i + 1), :] = cb
        ct_ref[0, :, blk * i:blk * (i + 1)] = cbt
        carry = cb[blk - 1:blk, :]
        carry_t = cbt[:, blk - 1:blk]


def _cumsum(lf):
    b, t, h = lf.shape
    return pl.pallas_call(
        functools.partial(_cumsum_kernel, blk=min(256, t)),
        grid=(b,),
        in_specs=[pl.BlockSpec((1, t, h), lambda i: (i, 0, 0))],
        out_specs=[pl.BlockSpec((1, t, h), lambda i: (i, 0, 0)), pl.BlockSpec((1, h, t), lambda i: (i, 0, 0))],
        out_shape=[jax.ShapeDtypeStruct((b, t, h), F32), jax.ShapeDtypeStruct((b, h, t), F32)],
        compiler_params=_cparams(1),
        name='forget_cumsum',
    )(lf)


def _online(state, s, pv):
    m_i, l_i, acc = state
    m_new = jnp.maximum(m_i, jnp.max(s, axis=-1, keepdims=True))
    alpha = jnp.exp(m_i - m_new)
    p = jnp.exp(s - m_new)
    l_new = alpha * l_i + jnp.sum(p, axis=-1, keepdims=True)
    acc_new = alpha * acc + pv(p.astype(BF16))
    return m_new, l_new, acc_new


def _init_state(rows, dv):
    return (jnp.full((rows, 1), NEG, F32), jnp.zeros((rows, 1), F32), jnp.zeros((rows, dv), F32))


def _causal_rows(s, i, j, tq, tk):
    qpos = i * tq + _iota(s.shape, 0) % tq
    kpos = j * tk + _iota(s.shape, 1)
    return jnp.where(kpos <= qpos, s, NEG)


def _compact_pairs(o, g, tq, o_ref, first_pair):
    lo = _iota((tq, LANES), 1) < HEAD_DIM
    for m in range(GROUP // 2):
        a, b = o[2 * m * tq:(2 * m + 1) * tq], o[(2 * m + 1) * tq:(2 * m + 2) * tq]
        pair = jnp.where(lo, a, pltpu.roll(b, HEAD_DIM, 1)) if g == 0 else jnp.where(lo, pltpu.roll(a, HEAD_DIM, 1), b)
        o_ref[0, :, LANES * (first_pair + m):LANES * (first_pair + m + 1)] = pair


def _fox_p_kernel(q_ref, k_ref, v_ref, c_ref, ct_ref, o_ref, *, tq, tk, nc):
    i = pl.program_id(1)
    nfull = (i * tq) // tk
    for g in range(FOX_KV_HEADS):
        heads = range(GROUP * g, GROUP * (g + 1))
        qg = _rows([(q_ref[0, :, LANES * h:LANES * (h + 1)] * FOX_SCALE).astype(BF16) for h in heads])
        cq = _rows([c_ref[0, :, h:h + 1] for h in heads])

        def chunk(j, st, masked, qg=qg, cq=cq, heads=heads):
            kc = k_ref[0, pl.ds(j * tk, tk), :].astype(BF16)
            vc = v_ref[0, pl.ds(j * tk, tk), :].astype(BF16)
            ck = _rows([jnp.broadcast_to(ct_ref[0, pl.ds(h * nc + j, 1), :], (tq, tk)) for h in heads])
            s = _dot_nt(qg, kc) + (cq - ck)
            if masked:
                s = _causal_rows(s, i, j, tq, tk)
            return _online(st, s, lambda p: _dot(p, vc))

        st = lax.fori_loop(0, nfull, functools.partial(chunk, masked=False), _init_state(GROUP * tq, LANES))
        _, l_i, acc = chunk(nfull, st, True)
        _compact_pairs(acc / l_i, g, tq, o_ref, g * (GROUP // 2))


def _fox_prompt(h3, c, ct, tq, tk):
    b, t, _ = h3.shape
    nc = t // tk
    ctr = ct.reshape(b, FOX_HEADS * nc, tk)
    return pl.pallas_call(
        functools.partial(_fox_p_kernel, tq=tq, tk=tk, nc=nc),
        grid=(b, t // tq),
        in_specs=[pl.BlockSpec((1, tq, N_HEADS * LANES), lambda bi, i: (bi, i, _blk('fox_q'))),
                  pl.BlockSpec((1, t, LANES), lambda bi, i: (bi, 0, _blk('fox_k'))),
                  pl.BlockSpec((1, t, LANES), lambda bi, i: (bi, 0, _blk('fox_v'))),
                  pl.BlockSpec((1, tq, FOX_HEADS), lambda bi, i: (bi, i, 0)),
                  pl.BlockSpec((1, FOX_HEADS * nc, tk), lambda bi, i: (bi, 0, 0))],
        out_specs=pl.BlockSpec((1, tq, FOX_W), lambda bi, i: (bi, i, 0)),
        out_shape=jax.ShapeDtypeStruct((b, t, FOX_W), F32),
        compiler_params=_cparams(2),
        name='fox_prompt',
    )(h3, h3, h3, c, ctr)


def _mla_p_kernel(ql_ref, qr_ref, ckv_ref, kr_ref, wv_ref, o_ref, *, tq, tk):
    i = pl.program_id(1)
    nfull = (i * tq) // tk
    ql = _rows([ql_ref[0, :, MLA_KV_LORA * h:MLA_KV_LORA * (h + 1)].astype(BF16) for h in range(MLA_HEADS)])
    qr = _rows([qr_ref[0, :, LANES * h:LANES * (h + 1)].astype(BF16) for h in range(MLA_HEADS)])

    def chunk(j, st, masked):
        kc = ckv_ref[0, pl.ds(j * tk, tk), :].astype(BF16)
        kr = kr_ref[0, pl.ds(j * tk, tk), :].astype(BF16)
        s = _dot_nt(ql, kc) + _dot_nt(qr, kr)
        if masked:
            s = _causal_rows(s, i, j, tq, tk)
        return _online(st, s, lambda p: _dot(p, kc))

    st = lax.fori_loop(0, nfull, functools.partial(chunk, masked=False), _init_state(MLA_HEADS * tq, MLA_KV_LORA))
    _, l_i, acc = chunk(nfull, st, True)
    o = (acc / l_i).astype(BF16)
    for m in range(MLA_HEADS // 2):
        o_ref[0, :, LANES * m:LANES * (m + 1)] = (_dot(o[2 * m * tq:(2 * m + 1) * tq], wv_ref[2 * m])
                                                  + _dot(o[(2 * m + 1) * tq:(2 * m + 2) * tq], wv_ref[2 * m + 1]))


def _mla_prompt(qlat, qrope, ckv, kr, wv_slot, tq, tk):
    b, t, _ = qlat.shape
    return pl.pallas_call(
        functools.partial(_mla_p_kernel, tq=tq, tk=tk),
        grid=(b, t // tq),
        in_specs=[pl.BlockSpec((1, tq, qlat.shape[2]), lambda bi, i: (bi, i, 0)),
                  pl.BlockSpec((1, tq, qrope.shape[2]), lambda bi, i: (bi, i, 0)),
                  pl.BlockSpec((1, t, MLA_KV_LORA), lambda bi, i: (bi, 0, 0)),
                  pl.BlockSpec((1, t, LANES), lambda bi, i: (bi, 0, 0)),
                  pl.BlockSpec(wv_slot.shape, lambda bi, i: (0, 0, 0))],
        out_specs=pl.BlockSpec((1, tq, MLA_W), lambda bi, i: (bi, i, 0)),
        out_shape=jax.ShapeDtypeStruct((b, t, MLA_W), F32),
        compiler_params=_cparams(2),
        name='mla_prompt',
    )(qlat, qrope, ckv, kr, wv_slot)


def _sort_key(score):
    b = lax.bitcast_convert_type(jnp.where(score == 0.0, 0.0, score), I32)
    return b ^ ((b >> 31) & 0x7FFFFFFF)


def _select_bias(keys_ref, write, topk, n_used):
    nc, rows, tk = keys_ref.shape
    kf = float(topk)
    rb = rows

    def counts(preds):
        out = []
        for r0 in range(0, rows, rb):
            def body(j, accs, r0=r0):
                x = keys_ref[j, r0:r0 + rb, :]
                new = []
                for acc, pred in zip(accs, preds):
                    w = jnp.where(pred(x, j, r0), 1.0, 0.0)
                    new.append(acc + sum(w[:, LANES * c:LANES * (c + 1)] for c in range(tk // LANES)))
                return tuple(new)
            accs = lax.fori_loop(0, n_used, body, tuple(jnp.zeros((rb, LANES), F32) for _ in preds))
            out.append([jnp.sum(a, axis=-1, keepdims=True) for a in accs])
        return [_rows([o[n] for o in out]) for n in range(len(preds))]

    def count(pred):
        return counts([pred])[0]

    def ge(cand):
        return lambda x, j, r0: x >= cand[r0:r0 + rb]

    cur = jnp.where(count(ge(jnp.zeros((rows, 1), I32))) >= kf, 0, INT_MIN).astype(I32)

    def vbody(b, cur):
        cand = cur | jnp.left_shift(jnp.int32(1), 30 - b)
        return jnp.where(count(ge(cand)) >= kf, cand, cur)

    thr = lax.fori_loop(0, 31, vbody, cur)
    eq = lambda x, j, r0: x == thr[r0:r0 + rb]
    n_gt, n_eq = counts([lambda x, j, r0: x > thr[r0:r0 + rb], eq])
    need = kf - n_gt
    lane = _iota((rb, tk), 1)
    nbits = max(1, int(math.ceil(math.log2(nc * tk))))

    def tie_break():
        def ibody(b, p):
            cand = p + jnp.left_shift(jnp.int32(1), nbits - 1 - b)
            n_lt = count(lambda x, j, r0: (x == thr[r0:r0 + rb]) & (j * tk + lane < cand[r0:r0 + rb]))
            return jnp.where(n_lt < need, cand, p)
        return lax.fori_loop(0, nbits, ibody, jnp.zeros((rows, 1), I32))

    excess = jnp.max(jnp.where(thr > INT_MIN, n_eq - need, 0.0))
    p = lax.cond(excess > 0.0, tie_break, lambda: jnp.full((rows, 1), nc * tk, I32))
    lane_all = _iota((rows, tk), 1)

    def wbody(j, carry):
        x = keys_ref[j]
        sel = (x > INT_MIN) & ((x > thr) | ((x == thr) & (j * tk + lane_all <= p)))
        write(j, jnp.where(sel, 0.0, NEG))
        return carry

    lax.fori_loop(0, n_used, wbody, 0)


def _dsa_p_kernel(iq_ref, iw_ref, ik_ref, q_ref, k_ref, v_ref, o_ref, keys_ref, bias_ref, *, tq, tk, topk):
    i = pl.program_id(1)
    n_used = (i * tq) // tk + 1
    iq = _rows([iq_ref[0, :, LANES * h:LANES * (h + 1)].astype(BF16) for h in range(IDX_HEADS)])
    iw = _rows([iw_ref[0, :, h:h + 1] for h in range(IDX_HEADS)])

    def ibody(j, carry):
        w = jnp.maximum(_dot_nt(iq, ik_ref[0, pl.ds(j * tk, tk), :].astype(BF16)), 0.0) * iw
        score = w[0:tq]
        for h in range(1, IDX_HEADS):
            score = score + w[h * tq:(h + 1) * tq]
        qpos = i * tq + _iota((tq, tk), 0)
        kpos = j * tk + _iota((tq, tk), 1)
        keys_ref[j] = jnp.where(kpos <= qpos, _sort_key(score), INT_MIN)
        return carry

    lax.fori_loop(0, n_used, ibody, 0)

    def write(j, bias):
        bias_ref[j] = bias

    _select_bias(keys_ref, write, topk, n_used)

    for g in range(DSA_KV_HEADS):
        qg = _rows([q_ref[0, :, LANES * h:LANES * (h + 1)].astype(BF16) for h in range(GROUP * g, GROUP * (g + 1))])

        def chunk(j, st, qg=qg):
            kc = k_ref[0, pl.ds(j * tk, tk), :].astype(BF16)
            vc = v_ref[0, pl.ds(j * tk, tk), :].astype(BF16)
            bias = bias_ref[j]
            return _online(st, _dot_nt(qg, kc) + _rows([bias] * GROUP), lambda p: _dot(p, vc))

        _, l_i, acc = lax.fori_loop(0, n_used, chunk, _init_state(GROUP * tq, LANES))
        _compact_pairs(acc / l_i, g, tq, o_ref, g * (GROUP // 2))


def _dsa_prompt(iq, iw, ik, dq, dk, h3, tq, tk):
    b, t, _ = dq.shape
    nc = t // tk
    topk = min(TOPK_MAX, t // 4)
    return pl.pallas_call(
        functools.partial(_dsa_p_kernel, tq=tq, tk=tk, topk=topk),
        grid=(b, t // tq),
        in_specs=[pl.BlockSpec((1, tq, N_HEADS * LANES), lambda bi, i: (bi, i, 0)),
                  pl.BlockSpec((1, tq, IDX_HEADS), lambda bi, i: (bi, i, 0)),
                  pl.BlockSpec((1, t, LANES), lambda bi, i: (bi, 0, 0)),
                  pl.BlockSpec((1, tq, N_HEADS * LANES), lambda bi, i: (bi, i, 0)),
                  pl.BlockSpec((1, t, LANES), lambda bi, i: (bi, 0, 0)),
                  pl.BlockSpec((1, t, LANES), lambda bi, i: (bi, 0, _blk('dsa_v')))],
        out_specs=pl.BlockSpec((1, tq, DSA_W), lambda bi, i: (bi, i, 0)),
        out_shape=jax.ShapeDtypeStruct((b, t, DSA_W), F32),
        scratch_shapes=[pltpu.VMEM((nc, tq, tk), I32), pltpu.VMEM((nc, tq, tk), F32)],
        compiler_params=_cparams(2),
        name='dsa_prompt',
    )(iq, iw, ik, dq, dk, h3)


def _mem_kernel(q_ref, k_ref, v_ref, o_ref):
    tq = q_ref.shape[1]
    half = _iota((k_ref.shape[-2], LANES), 1) // HEAD_DIM
    for m in range(MEM_HEADS // 2):
        qp = q_ref[0, :, LANES * m:LANES * (m + 1)].astype(BF16)
        kp = k_ref[:, LANES * m:LANES * (m + 1)] if k_ref.ndim == 2 else k_ref[0, :, LANES * m:LANES * (m + 1)]
        vp = v_ref[:, LANES * m:LANES * (m + 1)] if v_ref.ndim == 2 else v_ref[0, :, LANES * m:LANES * (m + 1)]
        pair = jnp.zeros((tq, LANES), F32)
        for s in range(2):
            kc = jnp.where(half == s, kp, 0.0).astype(BF16)
            vc = jnp.where(half == s, vp, 0.0).astype(BF16)
            sc = _dot_nt(qp, kc) * MEM_SCALE
            p = jnp.exp(sc - jnp.max(sc, axis=-1, keepdims=True))
            p = p / jnp.sum(p, axis=-1, keepdims=True)
            pair = pair + _dot(p.astype(BF16), vc)
        o_ref[0, :, LANES * m:LANES * (m + 1)] = pair


def _mem_attention(q_arr, q_spec, k_arr, k_spec, v_arr, v_spec, grid, out_shape, out_spec):
    return pl.pallas_call(
        _mem_kernel, grid=grid, in_specs=[q_spec, k_spec, v_spec], out_specs=out_spec,
        out_shape=jax.ShapeDtypeStruct(out_shape, F32), compiler_params=_cparams(len(grid)), name='mem_attention',
    )(q_arr, k_arr, v_arr)


def _finish_kernel(x_ref, fo_ref, mo_ref, do_ref, eo_ref, fz_ref, mz_ref, dz_ref, ez_ref, gt_ref,
                   wb_ref, wo_ref, g_ref, b_ref, o_ref, *, alpha):
    offs = np.cumsum((0,) + BRANCH_WIDTHS).tolist()
    d = x_ref.shape[-1]
    y = jnp.zeros(x_ref.shape, F32)
    for n, (o_r, z_r) in enumerate(((fo_ref, fz_ref), (mo_ref, mz_ref), (do_ref, dz_ref), (eo_ref, ez_ref))):
        z = z_r[...]
        u = o_r[...] * (z * jax.nn.sigmoid(z))
        pb = _dot(u.astype(BF16), wb_ref[offs[n]:offs[n + 1], :])
        y = y + jax.nn.sigmoid(gt_ref[:, d * n:d * (n + 1)]) * pb
    r = alpha * x_ref[...] + _dot(y.astype(BF16), wo_ref[...])
    mu = jnp.mean(r, axis=-1, keepdims=True)
    var = jnp.mean(jnp.square(r - mu), axis=-1, keepdims=True)
    o_ref[...] = (r - mu) * lax.rsqrt(var + LN_EPS) * g_ref[...] + b_ref[...]


def _finish(x2, h2, fo, mo, do, eo, wb, wo, g, b, tr, alpha):
    n, d = x2.shape
    row = lambda w: pl.BlockSpec((tr, w), lambda i: (i, 0))
    hcol = lambda name: pl.BlockSpec((tr, _LAYOUT[name][1]), lambda i, name=name: (i, _blk(name)))
    full = lambda a: pl.BlockSpec(a.shape, lambda i: (0,) * a.ndim)
    return pl.pallas_call(
        functools.partial(_finish_kernel, alpha=alpha),
        grid=(n // tr,),
        in_specs=[row(d), row(FOX_W), row(MLA_W), row(DSA_W), row(MEM_W),
                  hcol('fox_z'), hcol('mla_z'), hcol('dsa_z'), hcol('mem_z'), hcol('gates'),
                  full(wb), full(wo), full(g), full(b)],
        out_specs=row(d),
        out_shape=jax.ShapeDtypeStruct((n, d), F32),
        compiler_params=_cparams(1),
        name='finish',
    )(x2, fo, mo, do, eo, h2, h2, h2, h2, h2, wb, wo, g, b)


def _paged_fetch(pt_ref, srcs, bufs, sem, layer, pg, page_of):
    b, p = pl.program_id(0), pl.program_id(1)
    nb, steps = pl.num_programs(0), pl.num_programs(1)
    step = b * steps + p
    slot = step % 2

    def copies(bb, pp, sl):
        return [pltpu.make_async_copy(src.at[pt_ref[bb, page_of(pp, j)], layer], buf.at[sl, j], sem.at[n, sl])
                for n, (src, buf) in enumerate(zip(srcs, bufs)) for j in range(pg)]

    @pl.when(step == 0)
    def _():
        for c in copies(b, p, slot):
            c.start()

    @pl.when(step + 1 < nb * steps)
    def _():
        wrap = p + 1 == steps
        for c in copies(jnp.where(wrap, b + 1, b), jnp.where(wrap, 0, p + 1), 1 - slot):
            c.start()

    for c in copies(b, p, slot):
        c.wait()
    return [[buf.at[slot, j] for j in range(pg)] for buf in bufs]


def _page_scratch(caches, pg):
    return ([pltpu.VMEM((2, pg) + tuple(c.shape[2:]), c.dtype) for c in caches]
            + [pltpu.SemaphoreType.DMA((len(caches), 2))])


_HBM = pl.BlockSpec(memory_space=pl.ANY)


def _per_b(a):
    return pl.BlockSpec((1,) + a.shape[1:], lambda b, p, pt: (b,) + (0,) * (a.ndim - 1))


def _scratch_init(m_s, l_s, acc_s):
    m_s[...] = jnp.full(m_s.shape, NEG, F32)
    l_s[...] = jnp.zeros(l_s.shape, F32)
    acc_s[...] = jnp.zeros(acc_s.shape, F32)


def _scratch_update(m_s, l_s, acc_s, s, pv):
    m_new, l_new, acc_new = _online((m_s[...], l_s[...], acc_s[...]), s, pv)
    m_s[...], l_s[...], acc_s[...] = m_new, l_new, acc_new


def _chain_updates(m_s, l_s, acc_s, logit_pages, pv_of, pages):
    n = m_s.shape[0]
    per = len(pages) // n
    for c in range(n):
        sl = slice(c * per, (c + 1) * per)
        _scratch_update(m_s.at[c], l_s.at[c], acc_s.at[c], jnp.concatenate(logit_pages[sl], axis=1), pv_of(pages[sl]))


def _chain_merge(m_s, l_s, acc_s):
    m = m_s[0]
    for c in range(1, m_s.shape[0]):
        m = jnp.maximum(m, m_s[c])
    num = sum(jnp.exp(m_s[c] - m) * acc_s[c] for c in range(m_s.shape[0]))
    den = sum(jnp.exp(m_s[c] - m) * l_s[c] for c in range(m_s.shape[0]))
    return num / den


def _chain_scratch(rows, dv, pg):
    n = 2 if pg % 2 == 0 else 1
    return [pltpu.VMEM((n, rows, 1), F32), pltpu.VMEM((n, rows, 1), F32), pltpu.VMEM((n, rows, dv), F32)]


def _paged_pv(vts):
    def pv(p):
        return sum(_dot_nt(p[:, PAGE_SIZE * j:PAGE_SIZE * (j + 1)], vts[j][...].astype(BF16)) for j in range(len(vts)))
    return pv


def _fox_s_kernel(pt_ref, q_ref, lrows_ref, kn_ref, vn_ref, lfn_ref, ck_ref, cv_ref, clf_ref, o_ref,
                  m_s, l_s, acc_s, carry_s, kbuf, vbuf, lbuf, sem, *, pg, n_tok, layer, page_of):
    kps, vps, lps = _paged_fetch(pt_ref, (ck_ref, cv_ref, clf_ref), (kbuf, vbuf, lbuf), sem, layer, pg, page_of)
    p = pl.program_id(1)
    rows = q_ref.shape[1]
    q = (q_ref[0] * FOX_SCALE).astype(BF16)
    lane = _iota((rows, PAGE_SIZE), 1)
    t_row = _iota((rows, PAGE_SIZE), 0) // N_HEADS
    sp = _iota((PAGE_SIZE, 2 * PAGE_SIZE), 0)
    cc = _iota((PAGE_SIZE, 2 * PAGE_SIZE), 1)
    msk = jnp.where(((cc < PAGE_SIZE) & (sp >= cc)) | ((cc >= PAGE_SIZE) & (sp > cc - PAGE_SIZE)), 1.0, 0.0).astype(BF16)
    rowc = jnp.sum(jnp.where(lane > t_row, lrows_ref[0], 0.0), axis=-1, keepdims=True)

    def logits(kts, lfts, carry):
        out = []
        nh = lfts[0].shape[0]
        sfx_all = sum(_dot(part, msk) for part in _split3(_rows(lfts)))
        for n, kt in enumerate(kts):
            sfx = sfx_all[nh * n:nh * (n + 1)]
            bias8 = sfx[:, PAGE_SIZE:] + carry
            carry = carry + sfx[:, 0:1]
            out.append(_dot(q, kt.astype(BF16)) + (_rows([bias8] * n_tok) - rowc))
        return out, carry

    @pl.when(p == 0)
    def _():
        _scratch_init(m_s, l_s, acc_s)
        (s,), carry = logits([kn_ref[0]], [lfn_ref[0]], jnp.zeros(carry_s.shape, F32))
        carry_s[...] = carry
        _scratch_update(m_s.at[0], l_s.at[0], acc_s.at[0], jnp.where(lane <= t_row, s, NEG), _paged_pv([vn_ref.at[0]]))

    ss, carry = logits([r[...] for r in kps], [r[...] for r in lps], carry_s[...])
    carry_s[...] = carry
    _chain_updates(m_s, l_s, acc_s, ss, _paged_pv, vps)

    @pl.when(p == pl.num_programs(1) - 1)
    def _():
        o_ref[0] = _chain_merge(m_s, l_s, acc_s)


def _fox_sample(pt, q, lrows, knt, vnt, lfnt, ckt, cvt, clft, layer, pg):
    bs, npages = pt.shape
    rows = q.shape[1]
    rev = lambda p, j: npages - 1 - (p * pg + j)
    grid_spec = pltpu.PrefetchScalarGridSpec(
        num_scalar_prefetch=1, grid=(bs, npages // pg),
        in_specs=[_per_b(q), _per_b(lrows), _per_b(knt), _per_b(vnt), _per_b(lfnt), _HBM, _HBM, _HBM],
        out_specs=pl.BlockSpec((1, rows, LANES), lambda b, p, pt: (b, 0, 0)),
        scratch_shapes=_chain_scratch(rows, LANES, pg) + [pltpu.VMEM((FOX_HEADS, 1), F32)]
        + _page_scratch((ckt, cvt, clft), pg))
    return pl.pallas_call(
        functools.partial(_fox_s_kernel, pg=pg, n_tok=rows // N_HEADS, layer=layer, page_of=rev),
        grid_spec=grid_spec, out_shape=jax.ShapeDtypeStruct((bs, rows, LANES), F32),
        compiler_params=_cparams(2), name='fox_sample',
    )(pt, q, lrows, knt, vnt, lfnt, ckt, cvt, clft)


def _mla_s_kernel(pt_ref, ql_ref, qr_ref, cn_ref, rn_ref, cc_ref, cr_ref, o_ref, m_s, l_s, acc_s, cbuf, rbuf, sem,
                  *, pg, layer, page_of):
    cps, rps = _paged_fetch(pt_ref, (cc_ref, cr_ref), (cbuf, rbuf), sem, layer, pg, page_of)
    p = pl.program_id(1)
    rows = ql_ref.shape[1]
    ql = ql_ref[0].astype(BF16)
    qr = qr_ref[0].astype(BF16)
    lane = _iota((rows, PAGE_SIZE), 1)
    t_row = _iota((rows, PAGE_SIZE), 0) // N_HEADS

    def logits(c, rt):
        return _dot_nt(ql, c.astype(BF16)) + _dot(qr, rt.astype(BF16))

    def pv_of(cs):
        def pv(pr):
            return sum(_dot(pr[:, PAGE_SIZE * j:PAGE_SIZE * (j + 1)], cs[j][...].astype(BF16)) for j in range(len(cs)))
        return pv

    @pl.when(p == 0)
    def _():
        _scratch_init(m_s, l_s, acc_s)
        s = logits(cn_ref[0], rn_ref[0])
        _scratch_update(m_s.at[0], l_s.at[0], acc_s.at[0], jnp.where(lane <= t_row, s, NEG), pv_of([cn_ref.at[0]]))

    _chain_updates(m_s, l_s, acc_s, [logits(cps[j][...], rps[j][...]) for j in range(pg)], pv_of, cps)

    @pl.when(p == pl.num_programs(1) - 1)
    def _():
        o_ref[0] = _chain_merge(m_s, l_s, acc_s)


def _mla_sample(pt, ql, qr, cn, rnt, cckv, ckrt, layer, pg):
    bs, npages = pt.shape
    rows = ql.shape[1]
    fwd = lambda p, j: p * pg + j
    grid_spec = pltpu.PrefetchScalarGridSpec(
        num_scalar_prefetch=1, grid=(bs, npages // pg),
        in_specs=[_per_b(ql), _per_b(qr), _per_b(cn), _per_b(rnt), _HBM, _HBM],
        out_specs=pl.BlockSpec((1, rows, MLA_KV_LORA), lambda b, p, pt: (b, 0, 0)),
        scratch_shapes=_chain_scratch(rows, MLA_KV_LORA, pg) + _page_scratch((cckv, ckrt), pg))
    return pl.pallas_call(
        functools.partial(_mla_s_kernel, pg=pg, layer=layer, page_of=fwd),
        grid_spec=grid_spec, out_shape=jax.ShapeDtypeStruct((bs, rows, MLA_KV_LORA), F32),
        compiler_params=_cparams(2), name='mla_sample',
    )(pt, ql, qr, cn, rnt, cckv, ckrt)


def _mla_up_kernel(o_ref, wv_ref, out_ref):
    full = _dot(o_ref[0].astype(BF16), wv_ref[...])
    head = _iota(full.shape, 0) % MLA_HEADS
    keep = (_iota(full.shape, 1) // MLA_V) == head
    out_ref[0] = jnp.where(keep, full, 0.0)


def _mla_up(olat, wv_all):
    bs, rows, _ = olat.shape
    return pl.pallas_call(
        _mla_up_kernel, grid=(bs,),
        in_specs=[pl.BlockSpec((1, rows, MLA_KV_LORA), lambda b: (b, 0, 0)), pl.BlockSpec(wv_all.shape, lambda b: (0, 0))],
        out_specs=pl.BlockSpec((1, rows, MLA_W), lambda b: (b, 0, 0)),
        out_shape=jax.ShapeDtypeStruct((bs, rows, MLA_W), F32), compiler_params=_cparams(1), name='mla_up_sample',
    )(olat, wv_all)


def _dsa_idx_s_kernel(pt_ref, iq_ref, iw_ref, ikn_ref, ci_ref, bias_ref, keys_ref, ibuf, sem,
                      *, pg, topk, n_tok, layer, page_of):
    (ips,) = _paged_fetch(pt_ref, (ci_ref,), (ibuf,), sem, layer, pg, page_of)
    p = pl.program_id(1)
    rows = iq_ref.shape[1]
    steps = keys_ref.shape[0] - 1
    trows = keys_ref.shape[1]
    iq = iq_ref[0].astype(BF16)
    iw = iw_ref[0]
    tok = _iota((trows, PAGE_SIZE), 0)
    lane = _iota((trows, PAGE_SIZE), 1)
    pad_rows = jnp.zeros((trows - n_tok, PAGE_SIZE), F32)

    def score_keys(ikt):
        w = jnp.maximum(_dot(iq, ikt.astype(BF16)), 0.0) * iw
        tot = jnp.sum(w.reshape(n_tok, IDX_HEADS, PAGE_SIZE), axis=1)
        return jnp.where(tok < n_tok, _sort_key(jnp.concatenate([tot, pad_rows], axis=0)), INT_MIN)

    @pl.when(p == 0)
    def _():
        keys_ref[steps] = jnp.full(keys_ref.shape[1:], INT_MIN, I32)
        keys_ref[steps, :, 0:PAGE_SIZE] = jnp.where(lane <= tok, score_keys(ikn_ref[0]), INT_MIN)

    for j in range(pg):
        keys_ref[p, :, PAGE_SIZE * j:PAGE_SIZE * (j + 1)] = score_keys(ips[j][...])

    def write(j, bias):
        bias_ref[0, j] = _rows([jnp.broadcast_to(bias[t:t + 1, :], (IDX_HEADS, bias.shape[1])) for t in range(n_tok)])

    @pl.when(p == steps - 1)
    def _():
        _select_bias(keys_ref, write, topk, steps + 1)


def _dsa_idx_sample(pt, iq, iw, iknt, cidxt, layer, pg, topk):
    bs, npages = pt.shape
    rows = iq.shape[1]
    fwd = lambda p, j: p * pg + j
    steps = npages // pg
    grid_spec = pltpu.PrefetchScalarGridSpec(
        num_scalar_prefetch=1, grid=(bs, steps),
        in_specs=[_per_b(iq), _per_b(iw), _per_b(iknt), _HBM],
        out_specs=pl.BlockSpec((1, steps + 1, rows, pg * PAGE_SIZE), lambda b, p, pt: (b, 0, 0, 0)),
        scratch_shapes=[pltpu.VMEM((steps + 1, SUBLANES, pg * PAGE_SIZE), I32)] + _page_scratch((cidxt,), pg))
    assert rows // IDX_HEADS <= SUBLANES
    return pl.pallas_call(
        functools.partial(_dsa_idx_s_kernel, pg=pg, topk=topk, n_tok=rows // IDX_HEADS, layer=layer, page_of=fwd),
        grid_spec=grid_spec, out_shape=jax.ShapeDtypeStruct((bs, steps + 1, rows, pg * PAGE_SIZE), F32),
        compiler_params=_cparams(2), name='dsa_index_sample',
    )(pt, iq, iw, iknt, cidxt)


def _dsa_s_kernel(pt_ref, q_ref, kn_ref, vn_ref, bn_ref, bp_ref, ck_ref, cv_ref, o_ref, m_s, l_s, acc_s, kbuf, vbuf, sem,
                  *, pg, layer, page_of):
    kps, vps = _paged_fetch(pt_ref, (ck_ref, cv_ref), (kbuf, vbuf), sem, layer, pg, page_of)
    p = pl.program_id(1)
    q = q_ref[0].astype(BF16)

    @pl.when(p == 0)
    def _():
        _scratch_init(m_s, l_s, acc_s)
        _scratch_update(m_s.at[0], l_s.at[0], acc_s.at[0],
                        _dot(q, kn_ref[0].astype(BF16)) + bn_ref[0, :, 0:PAGE_SIZE], _paged_pv([vn_ref.at[0]]))

    _chain_updates(m_s, l_s, acc_s, [_dot(q, kps[j][...].astype(BF16)) + bp_ref[0, :, PAGE_SIZE * j:PAGE_SIZE * (j + 1)]
                                     for j in range(pg)], _paged_pv, vps)

    @pl.when(p == pl.num_programs(1) - 1)
    def _():
        o_ref[0] = _chain_merge(m_s, l_s, acc_s)


def _dsa_sample(pt, q, knt, vnt, bias, ckt, cvt, layer, pg):
    bs, npages = pt.shape
    rows = q.shape[1]
    fwd = lambda p, j: p * pg + j
    grid_spec = pltpu.PrefetchScalarGridSpec(
        num_scalar_prefetch=1, grid=(bs, npages // pg),
        in_specs=[_per_b(q), _per_b(knt), _per_b(vnt),
                  pl.BlockSpec((None, 1, rows, pg * PAGE_SIZE), lambda b, p, pt: (b, npages // pg, 0, 0)),
                  pl.BlockSpec((None, 1, rows, pg * PAGE_SIZE), lambda b, p, pt: (b, p, 0, 0)), _HBM, _HBM],
        out_specs=pl.BlockSpec((1, rows, LANES), lambda b, p, pt: (b, 0, 0)),
        scratch_shapes=_chain_scratch(rows, LANES, pg) + _page_scratch((ckt, cvt), pg))
    return pl.pallas_call(
        functools.partial(_dsa_s_kernel, pg=pg, layer=layer, page_of=fwd),
        grid_spec=grid_spec, out_shape=jax.ShapeDtypeStruct((bs, rows, LANES), F32),
        compiler_params=_cparams(2), name='dsa_sample',
    )(pt, q, knt, vnt, bias, bias, ckt, cvt)


def _pad_cols(w):
    zeros = lambda n: jnp.zeros(w.shape[:2] + (n,), w.dtype)
    parts = []
    for name, pad, slotting in _GROUPS:
        if name == 'pad':
            parts.append(zeros(pad))
            continue
        src, real = _SRC[name], _WID[name]
        if slotting is None:
            parts.append(w[:, :, src:src + real])
            if pad > real:
                parts.append(zeros(pad - real))
            continue
        for h in range(N_HEADS):
            head = w[:, :, src + HEAD_DIM * h:src + HEAD_DIM * (h + 1)]
            first = slotting == 'lo' or h // GROUP == 0
            parts.extend([head, zeros(HEAD_DIM)] if first else [zeros(HEAD_DIM), head])
    return jnp.concatenate(parts, axis=-1)


def _rot_tables(pos, half, width, reps):
    inv_freq = jnp.exp(-math.log(ROPE_THETA) * jnp.arange(half, dtype=F32) / half)
    ang = pos.astype(F32)[:, None] * inv_freq[None, :]
    cos, sin = jnp.cos(ang), jnp.sin(ang)
    n = pos.shape[0]
    cos_h = jnp.concatenate([cos, cos, jnp.ones((n, width - 2 * half), F32)], axis=-1)
    sin_h = jnp.concatenate([-sin, sin, jnp.zeros((n, width - 2 * half), F32)], axis=-1)
    return jnp.tile(cos_h, (1, reps)), jnp.tile(sin_h, (1, reps))


def _new_page_t(a):
    return jnp.pad(jnp.swapaxes(a, 1, 2), ((0, 0), (0, 0), (0, PAGE_SIZE - a.shape[1])))


def kernel(x_prompt, x_sample, cache_fox_k, cache_fox_v, cache_fox_logf, cache_mla_ckv, cache_mla_krope,
           cache_dsa_k, cache_dsa_v, cache_dsa_kidx, cache_mem_k, cache_mem_v, page_table, mem_prompt,
           w_in, b_forget, mla_q_norm, mla_kv_norm, w_mla_q_up, w_mla_kv_up, w_mem_kv, w_branch, w_out,
           ln_gain, ln_bias):
    b, t, d = x_prompt.shape
    bs, ts, _ = x_sample.shape
    depth = w_in.shape[0]
    mem_len = mem_prompt.shape[1]
    npages = page_table.shape[1]
    past_len = npages * PAGE_SIZE
    pool = cache_fox_k.shape[0]
    rows = ts * N_HEADS
    alpha = (2 * depth) ** 0.25
    assert FOX_HEADS == MLA_HEADS == DSA_HEADS == IDX_HEADS == N_HEADS and ts <= PAGE_SIZE

    tq = min(128, t)
    tq_dsa = min(256, t)
    tk = min(512, t)
    tk_mla = min(512, t)
    tq_fox = min(256, t)
    tr = min(256, t)
    pg = math.gcd(npages, 32)
    n_s = bs * ts
    topk_s = min(TOPK_MAX, (past_len + ts) // 4)

    w_pad = _pad_cols(w_in).astype(BF16)
    bf_pad = jnp.pad(b_forget, ((0, 0), (0, LANES - FOX_HEADS)))[:, None, :]
    qn_pad = jnp.pad(mla_q_norm, ((0, 0), (0, 512 - MLA_Q_LORA)))[:, None, :]
    kvn = mla_kv_norm[:, None, :]
    wq = w_mla_q_up.reshape(depth, MLA_Q_LORA, MLA_HEADS, MLA_NOPE + MLA_ROPE)
    wq_pad = lambda a: jnp.pad(a, ((0, 0), (0, 512 - MLA_Q_LORA), (0, 0), (0, LANES - a.shape[-1]))
                               ).reshape(depth, 512, MLA_HEADS * LANES).astype(BF16)
    wq_nope, wq_rope = wq_pad(wq[..., :MLA_NOPE]), wq_pad(wq[..., MLA_NOPE:])
    wkt = jnp.pad(jnp.transpose(w_mla_kv_up[..., :MLA_NOPE], (0, 2, 3, 1)), ((0, 0), (0, 0), (0, LANES - MLA_NOPE), (0, 0))
                  ).reshape(depth, MLA_HEADS * LANES, MLA_KV_LORA).astype(BF16)
    wv = jnp.transpose(w_mla_kv_up[..., MLA_NOPE:], (0, 2, 1, 3))
    wv_slot = jnp.stack([jnp.pad(wv[:, h], ((0, 0), (0, 0), ((h % 2) * MLA_V, (1 - h % 2) * MLA_V)))
                         for h in range(MLA_HEADS)], axis=1).astype(BF16)
    wv_all = jnp.transpose(wv, (0, 2, 1, 3)).reshape(depth, MLA_KV_LORA, MLA_W).astype(BF16)
    w_mem = w_mem_kv.astype(BF16)
    wb, wo = w_branch.astype(BF16), w_out.astype(BF16)
    ln_g, ln_b = ln_gain[:, None, :], ln_bias[:, None, :]

    pos_p = jnp.arange(t, dtype=jnp.int32)
    pos_s = jnp.tile(past_len + jnp.arange(ts, dtype=jnp.int32), bs)
    tabs_p = _rot_tables(pos_p, ROT_DIM // 2, HEAD_DIM, 2) + _rot_tables(pos_p, MLA_ROPE // 2, MLA_ROPE, 4)
    tabs_s = _rot_tables(pos_s, ROT_DIM // 2, HEAD_DIM, 2) + _rot_tables(pos_s, MLA_ROPE // 2, MLA_ROPE, 4)

    kv_t = lambda c: jnp.transpose(c, (0, 1, 3, 4, 2)).reshape(pool, depth, FOX_KV_HEADS * HEAD_DIM, PAGE_SIZE)
    c_fkt, c_fvt, c_dkt, c_dvt = kv_t(cache_fox_k), kv_t(cache_fox_v), kv_t(cache_dsa_k), kv_t(cache_dsa_v)
    c_lft = jnp.swapaxes(cache_fox_logf, 2, 3)
    c_krt = jnp.swapaxes(cache_mla_krope, 2, 3)
    c_ixt = jnp.swapaxes(cache_dsa_kidx, 2, 3)
    c_mk = cache_mem_k.reshape(bs, depth, mem_len, MEM_W)
    c_mv = cache_mem_v.reshape(bs, depth, mem_len, MEM_W)

    xp = x_prompt.reshape(b * t, d)
    xs = x_sample.reshape(n_s, d)
    mem2 = mem_prompt.reshape(b * mem_len, d)
    P = {k: [] for k in ('fox_k', 'fox_v', 'logf', 'ckv', 'kr', 'dsa_k', 'dsa_v', 'ik', 'mem_k', 'mem_v')}
    S = {k: [] for k in ('fox_k', 'fox_v', 'logf', 'ckv', 'kr', 'dsa_k', 'dsa_v', 'ik')}

    def col(h2, name, width=None):
        off = _LAYOUT[name][0]
        return h2[:, off:off + (width or _LAYOUT[name][1])]

    for l in range(depth):
        params = (bf_pad[l], qn_pad[l], kvn[l], wq_nope[l], wq_rope[l], wkt[l])

        h2 = _matmul(xp, w_pad[l], min(1024, b * t), PROJ_TN)
        pr = _prep(h2, tabs_p, params, tr)
        h3 = h2.reshape(b, t, EP)
        r3 = lambda a: a.reshape(b, t, a.shape[-1])
        c, ct = _cumsum(r3(pr['logf']))
        fox_o = _fox_prompt(h3, c, ct, tq_fox, tk)
        mla_o = _mla_prompt(r3(pr['qlat']), r3(pr['qrope']), r3(pr['ckv']), r3(pr['kr']), wv_slot[l], tq, tk_mla)
        dsa_o = _dsa_prompt(r3(pr['iq']), r3(pr['iw']), r3(pr['ik']), r3(pr['dq']), r3(pr['dk']), h3, tq_dsa, tk)
        mkv = _matmul(mem2, w_mem[l], min(1024, b * mem_len), 2 * MEM_W).reshape(b, mem_len, 2 * MEM_W)
        tqm = min(256, t)
        mem_o = _mem_attention(
            h3, pl.BlockSpec((1, tqm, MEM_W), lambda bi, i: (bi, i, _blk('mem_q'))),
            mkv, pl.BlockSpec((1, mem_len, MEM_W), lambda bi, i: (bi, 0, 0)),
            mkv, pl.BlockSpec((1, mem_len, MEM_W), lambda bi, i: (bi, 0, 1)),
            (b, t // tqm), (b, t, MEM_W), pl.BlockSpec((1, tqm, MEM_W), lambda bi, i: (bi, i, 0)))
        xp_new = _finish(xp, h2, fox_o.reshape(b * t, -1), mla_o.reshape(b * t, -1), dsa_o.reshape(b * t, -1),
                         mem_o.reshape(b * t, -1), wb[l], wo[l], ln_g[l], ln_b[l], tr, alpha)
        P['fox_k'].append(col(h2, 'fox_k')); P['fox_v'].append(col(h2, 'fox_v')); P['logf'].append(pr['logf'])
        P['ckv'].append(pr['ckv']); P['kr'].append(pr['kr'][:, :MLA_ROPE]); P['dsa_k'].append(pr['dk'])
        P['dsa_v'].append(col(h2, 'dsa_v')); P['ik'].append(pr['ik'][:, :IDX_DIM])
        P['mem_k'].append(mkv[..., :MEM_W]); P['mem_v'].append(mkv[..., MEM_W:])
        xp = xp_new

        g2 = _matmul(xs, w_pad[l], n_s, PROJ_TN)
        sr = _prep(g2, tabs_s, params, n_s)
        rows_of = lambda a: a.reshape(bs, rows, a.shape[-1] // N_HEADS)
        new3 = lambda a: a.reshape(bs, ts, a.shape[-1])
        fold = lambda o: jnp.where(((jnp.arange(rows) % N_HEADS) // GROUP == 0)[None, :, None],
                                   o[..., :HEAD_DIM], o[..., HEAD_DIM:]).reshape(n_s, N_HEADS * HEAD_DIM)
        lf_new = new3(sr['logf'])
        lrows = jnp.broadcast_to(jnp.transpose(lf_new, (0, 2, 1))[:, None], (bs, ts, FOX_HEADS, ts)).reshape(bs, rows, ts)
        lrows = jnp.pad(lrows, ((0, 0), (0, 0), (0, LANES - ts)))
        fox_s = fold(_fox_sample(page_table, rows_of(col(g2, 'fox_q')), lrows,
                                 _new_page_t(new3(col(g2, 'fox_k'))), _new_page_t(new3(col(g2, 'fox_v'))),
                                 _new_page_t(lf_new), c_fkt, c_fvt, c_lft, l, pg))
        olat = _mla_sample(page_table, rows_of(sr['qlat']), rows_of(sr['qrope'])[..., :MLA_ROPE],
                           _new_page_t(new3(sr['ckv'])).swapaxes(1, 2), _new_page_t(new3(sr['kr'][:, :MLA_ROPE])),
                           cache_mla_ckv, c_krt, l, pg)
        mla_s = _mla_up(olat, wv_all[l]).reshape(bs, rows, MLA_HEADS, MLA_V).sum(axis=2).reshape(n_s, MLA_W)
        bias = _dsa_idx_sample(page_table, rows_of(sr['iq'])[..., :IDX_DIM], sr['iw'].reshape(bs, rows, 1),
                               _new_page_t(new3(sr['ik'][:, :IDX_DIM])), c_ixt, l, pg, topk_s)
        dsa_s = fold(_dsa_sample(page_table, rows_of(sr['dq']), _new_page_t(new3(sr['dk'])),
                                 _new_page_t(new3(col(g2, 'dsa_v'))), bias, c_dkt, c_dvt, l, pg))
        mem_s = _mem_attention(
            new3(col(g2, 'mem_q')), pl.BlockSpec((1, ts, MEM_W), lambda bi: (bi, 0, 0)),
            c_mk, pl.BlockSpec((None, None, mem_len, MEM_W), lambda bi, l=l: (bi, l, 0, 0)),
            c_mv, pl.BlockSpec((None, None, mem_len, MEM_W), lambda bi, l=l: (bi, l, 0, 0)),
            (bs,), (bs, ts, MEM_W), pl.BlockSpec((1, ts, MEM_W), lambda bi: (bi, 0, 0))).reshape(n_s, MEM_W)
        xs_new = _finish(xs, g2, fox_s, mla_s, dsa_s, mem_s, wb[l], wo[l], ln_g[l], ln_b[l], n_s, alpha)
        S['fox_k'].append(col(g2, 'fox_k')); S['fox_v'].append(col(g2, 'fox_v')); S['logf'].append(sr['logf'])
        S['ckv'].append(sr['ckv']); S['kr'].append(sr['kr'][:, :MLA_ROPE]); S['dsa_k'].append(sr['dk'])
        S['dsa_v'].append(col(g2, 'dsa_v')); S['ik'].append(sr['ik'][:, :IDX_DIM])
        xs = xs_new

    def stack(lst, lead, tail):
        return jnp.stack([a.reshape(lead + tail) for a in lst], axis=1)

    kv2 = (FOX_KV_HEADS, HEAD_DIM)
    pl_, sl_ = (b, t), (bs, ts)
    return (xp.reshape(b, t, d), xs.reshape(bs, ts, d),
            stack(P['fox_k'], pl_, kv2), stack(P['fox_v'], pl_, kv2), stack(P['logf'], pl_, (FOX_HEADS,)),
            stack(P['ckv'], pl_, (MLA_KV_LORA,)), stack(P['kr'], pl_, (MLA_ROPE,)),
            stack(P['dsa_k'], pl_, kv2), stack(P['dsa_v'], pl_, kv2), stack(P['ik'], pl_, (IDX_DIM,)),
            stack(P['mem_k'], (b, mem_len), (MEM_HEADS, HEAD_DIM)), stack(P['mem_v'], (b, mem_len), (MEM_HEADS, HEAD_DIM)),
            stack(S['fox_k'], sl_, kv2), stack(S['fox_v'], sl_, kv2), stack(S['logf'], sl_, (FOX_HEADS,)),
            stack(S['ckv'], sl_, (MLA_KV_LORA,)), stack(S['kr'], sl_, (MLA_ROPE,)),
            stack(S['dsa_k'], sl_, kv2), stack(S['dsa_v'], sl_, kv2), stack(S['ik'], sl_, (IDX_DIM,)))
```

```python
import functools
import math

import numpy as np
import jax
import jax.numpy as jnp
from jax import lax
from jax.experimental import pallas as pl
from jax.experimental.pallas import tpu as pltpu

F32 = jnp.float32
BF16 = jnp.bfloat16
I32 = jnp.int32

HEAD_DIM = 64
ROT_DIM = HEAD_DIM // 4
ROPE_THETA = 500000.0
FOX_HEADS = 8
FOX_KV_HEADS = 2
MLA_HEADS = 8
MLA_NOPE = 64
MLA_ROPE = 32
MLA_V = 64
MLA_Q_LORA = 384
MLA_KV_LORA = 256
DSA_HEADS = 8
DSA_KV_HEADS = 2
IDX_HEADS = 8
IDX_DIM = 64
TOPK_MAX = 256
MEM_HEADS = 4
N_BRANCH = 4
PAGE_SIZE = 128
N_HEADS = 8
GROUP = N_HEADS // FOX_KV_HEADS
FOX_W = FOX_HEADS * HEAD_DIM
MLA_W = MLA_HEADS * MLA_V
DSA_W = DSA_HEADS * HEAD_DIM
MEM_W = MEM_HEADS * HEAD_DIM
BRANCH_WIDTHS = (FOX_W, MLA_W, DSA_W, MEM_W)
FOX_SCALE = HEAD_DIM ** -0.5
MLA_SCALE = (MLA_NOPE + MLA_ROPE) ** -0.5
DSA_SCALE = HEAD_DIM ** -0.5
MEM_SCALE = HEAD_DIM ** -0.5
IDX_SCALE = IDX_DIM ** -0.5
IDX_W_SCALE = IDX_HEADS ** -0.5
LN_EPS = 1e-5
RMS_EPS = 1e-6

LANES = 128
SUBLANES = 8
VMEM_LIMIT_BYTES = 48 * 2 ** 20

NEG = -1e30
INT_MIN = -2 ** 31

_IN_NAMES = ('fox_q', 'fox_k', 'fox_v', 'fox_f', 'fox_z', 'mla_qa', 'mla_kva', 'mla_kr', 'mla_z',
             'dsa_q', 'dsa_k', 'dsa_v', 'idx_q', 'idx_k', 'idx_w', 'dsa_z', 'mem_q', 'mem_z', 'gates')
_IN_SPLITS = (512, 128, 128, 8, 512, 384, 256, 32, 512, 512, 128, 128, 512, 64, 8, 512, 256, 256, 4096)
_SRC = dict(zip(_IN_NAMES, np.cumsum((0,) + _IN_SPLITS[:-1]).tolist()))
_WID = dict(zip(_IN_NAMES, _IN_SPLITS))
_GROUPS = (
    ('gates', 4096, None),
    ('fox_q', 1024, 'kv'), ('dsa_q', 1024, 'kv'), ('idx_q', 1024, 'lo'),
    ('fox_z', 512, None), ('mla_z', 512, None), ('dsa_z', 512, None), ('mla_qa', 512, None),
    ('mla_kva', 256, None), ('mem_q', 256, None), ('mem_z', 256, None),
    ('fox_k', 128, None), ('fox_v', 128, None), ('dsa_k', 128, None), ('dsa_v', 128, None),
    ('idx_k', 128, None), ('mla_kr', 128, None), ('fox_f', 128, None), ('idx_w', 128, None),
    ('pad', 256, None),
)
PROJ_TN = 1024


def _layout():
    off, out = 0, {}
    for name, pad, _ in _GROUPS:
        assert off % pad == 0
        out[name] = (off, pad)
        off += pad
    return out, off


_LAYOUT, EP = _layout()


def _blk(name):
    off, pad = _LAYOUT[name]
    return off // pad


def _cparams(n_grid):
    return pltpu.CompilerParams(dimension_semantics=('arbitrary',) * n_grid, vmem_limit_bytes=VMEM_LIMIT_BYTES)


def _dot(a, b):
    return jnp.dot(a, b, preferred_element_type=F32)


def _dot_nt(a, b):
    return lax.dot_general(a, b, (((1,), (1,)), ((), ())), preferred_element_type=F32)


def _dot_tn(a, b):
    return lax.dot_general(a, b, (((0,), (0,)), ((), ())), preferred_element_type=F32)


def _split3(x):
    hi = x.astype(BF16)
    r = x - hi.astype(F32)
    mid = r.astype(BF16)
    lo = (r - mid.astype(F32)).astype(BF16)
    return hi, mid, lo


def _iota(shape, dim):
    return lax.broadcasted_iota(I32, shape, dim)


def _rows(blocks):
    return jnp.concatenate(blocks, axis=0)


def _mm_kernel(x_ref, w_ref, o_ref, xb_ref):
    @pl.when(pl.program_id(1) == 0)
    def _():
        xb_ref[...] = x_ref[...].astype(BF16)

    o_ref[...] = _dot(xb_ref[...], w_ref[...])


def _matmul(x, w, tm, tn):
    m, k = x.shape
    n = w.shape[1]
    assert m % tm == 0 and n % tn == 0 and w.dtype == BF16
    return pl.pallas_call(
        _mm_kernel,
        grid=(m // tm, n // tn),
        in_specs=[pl.BlockSpec((tm, k), lambda i, j: (i, 0)), pl.BlockSpec((k, tn), lambda i, j: (0, j))],
        out_specs=pl.BlockSpec((tm, tn), lambda i, j: (i, j)),
        out_shape=jax.ShapeDtypeStruct((m, n), F32),
        scratch_shapes=[pltpu.VMEM((tm, k), BF16)],
        compiler_params=_cparams(2),
        name='proj_matmul',
    )(x, w)


def _rope(x, cos, sin, half, width):
    w = x.shape[-1]
    d = _iota(x.shape, 1) % width
    up = pltpu.roll(x, w - half, 1)
    dn = pltpu.roll(x, half, 1)
    partner = jnp.where(d < half, up, dn)
    return jnp.where(d < 2 * half, x * cos + partner * sin, x)


def _tile_lanes(x, n):
    return jnp.concatenate([x] * n, axis=-1) if n > 1 else x


def _prep_kernel(qa_ref, kva_ref, dq_ref, dk_ref, iq_ref, ik_ref, kr_ref, ff_ref, iw_ref,
                 ca_ref, sa_ref, cb_ref, sb_ref, bf_ref, qn_ref, kvn_ref, wqn_ref, wqr_ref, wkt_ref,
                 logf_ref, ckv_ref, kro_ref, dqo_ref, dko_ref, iqo_ref, iko_ref, iwo_ref, qlat_ref, qrope_ref):
    ca, sa, cb, sb = ca_ref[...], sa_ref[...], cb_ref[...], sb_ref[...]
    half_a, half_b = ROT_DIM // 2, MLA_ROPE // 2
    logf_ref[...] = jax.nn.log_sigmoid(ff_ref[...] + bf_ref[...])[:, :FOX_HEADS]
    kva = kva_ref[...]
    ckv_ref[...] = kva * lax.rsqrt(jnp.mean(kva * kva, axis=-1, keepdims=True) + RMS_EPS) * kvn_ref[...]
    kro_ref[...] = _rope(kr_ref[...], cb, sb, half_b, MLA_ROPE)
    dqo_ref[...] = (_rope(dq_ref[...], _tile_lanes(ca, N_HEADS), _tile_lanes(sa, N_HEADS), half_a, HEAD_DIM)
                    * DSA_SCALE).astype(BF16)
    dko_ref[...] = _rope(dk_ref[...], ca, sa, half_a, HEAD_DIM)
    iqo_ref[...] = (_rope(iq_ref[...], _tile_lanes(ca, N_HEADS), _tile_lanes(sa, N_HEADS), half_a, IDX_DIM)
                    * IDX_SCALE).astype(BF16)
    iko_ref[...] = _rope(ik_ref[...], ca, sa, half_a, IDX_DIM)
    iwo_ref[...] = (iw_ref[...] * IDX_W_SCALE)[:, :IDX_HEADS]
    qa = qa_ref[...]
    ms = jnp.sum(qa * qa, axis=-1, keepdims=True) * (1.0 / MLA_Q_LORA)
    qan = (qa * lax.rsqrt(ms + RMS_EPS) * qn_ref[...]).astype(BF16)
    q_nope = _dot(qan, wqn_ref[...]).astype(BF16)
    q_rope = _dot(qan, wqr_ref[...])
    qrope_ref[...] = (_rope(q_rope, _tile_lanes(cb, N_HEADS), _tile_lanes(sb, N_HEADS), half_b, MLA_ROPE)
                      * MLA_SCALE).astype(BF16)
    for h in range(MLA_HEADS):
        qlat_ref[:, MLA_KV_LORA * h:MLA_KV_LORA * (h + 1)] = (_dot(
            q_nope[:, LANES * h:LANES * (h + 1)], wkt_ref[LANES * h:LANES * (h + 1), :]) * MLA_SCALE).astype(BF16)


def _prep(h2, tabs, params, tr):
    n = h2.shape[0]
    nt = tabs[0].shape[0] // tr
    row = lambda name: pl.BlockSpec((tr, _LAYOUT[name][1]), lambda i, name=name: (i, _blk(name)))
    tab = pl.BlockSpec((tr, LANES), lambda i: (i % nt, 0))
    full = lambda a: pl.BlockSpec(a.shape, lambda i: (0,) * a.ndim)
    names = ('mla_qa', 'mla_kva', 'dsa_q', 'dsa_k', 'idx_q', 'idx_k', 'mla_kr', 'fox_f', 'idx_w')
    in_specs = [row(nm) for nm in names] + [tab] * 4 + [full(p) for p in params]
    outs = [('logf', FOX_HEADS), ('ckv', MLA_KV_LORA), ('kr', LANES), ('dq', N_HEADS * LANES), ('dk', LANES),
            ('iq', N_HEADS * LANES), ('ik', LANES), ('iw', IDX_HEADS), ('qlat', MLA_HEADS * MLA_KV_LORA),
            ('qrope', N_HEADS * LANES)]
    res = pl.pallas_call(
        _prep_kernel,
        grid=(n // tr,),
        in_specs=in_specs,
        out_specs=[pl.BlockSpec((tr, w), lambda i: (i, 0)) for _, w in outs],
        out_shape=[jax.ShapeDtypeStruct((n, w), BF16 if k in ('dq', 'iq', 'qlat', 'qrope') else F32) for k, w in outs],
        compiler_params=_cparams(1),
        name='prep_rows',
    )(*([h2] * len(names)), *tabs, *params)
    return dict(zip([k for k, _ in outs], res))


def _cumsum_kernel(lf_ref, c_ref, ct_ref, *, blk):
    t = lf_ref.shape[1]
    r, c = _iota((blk, blk), 0), _iota((blk, blk), 1)
    tri = jnp.where(c <= r, 1.0, 0.0).astype(BF16)
    trit = jnp.where(r <= c, 1.0, 0.0).astype(BF16)
    carry = jnp.zeros((1, FOX_HEADS), F32)
    carry_t = jnp.zeros((FOX_HEADS, 1), F32)
    for i in range(t // blk):
        parts = _split3(lf_ref[0, blk * i:blk * (i + 1), :])
        cb = sum(_dot(tri, p) for p in parts) + carry
        cbt = sum(_dot_tn(p, trit) for p in parts) + carry_t
        c_ref[0, blk * i:blk * (i + 1), :] = cb
        ct_ref[0, :, blk * i:blk * (i + 1)] = cbt
        carry = cb[blk - 1:blk, :]
        carry_t = cbt[:, blk - 1:blk]


def _cumsum(lf):
    b, t, h = lf.shape
    return pl.pallas_call(
        functools.partial(_cumsum_kernel, blk=min(256, t)),
        grid=(b,),
        in_specs=[pl.BlockSpec((1, t, h), lambda i: (i, 0, 0))],
        out_specs=[pl.BlockSpec((1, t, h), lambda i: (i, 0, 0)), pl.BlockSpec((1, h, t), lambda i: (i, 0, 0))],
        out_shape=[jax.ShapeDtypeStruct((b, t, h), F32), jax.ShapeDtypeStruct((b, h, t), F32)],
        compiler_params=_cparams(1),
        name='forget_cumsum',
    )(lf)


def _online(state, s, pv):
    m_i, l_i, acc = state
    m_new = jnp.maximum(m_i, jnp.max(s, axis=-1, keepdims=True))
    alpha = jnp.exp(m_i - m_new)
    p = jnp.exp(s - m_new)
    l_new = alpha * l_i + jnp.sum(p, axis=-1, keepdims=True)
    acc_new = alpha * acc + pv(p.astype(BF16))
    return m_new, l_new, acc_new


def _init_state(rows, dv):
    return (jnp.full((rows, 1), NEG, F32), jnp.zeros((rows, 1), F32), jnp.zeros((rows, dv), F32))


def _causal_rows(s, i, j, tq, tk):
    qpos = i * tq + _iota(s.shape, 0) % tq
    kpos = j * tk + _iota(s.shape, 1)
    return jnp.where(kpos <= qpos, s, NEG)


def _compact_pairs(o, g, tq, o_ref, first_pair):
    lo = _iota((tq, LANES), 1) < HEAD_DIM
    for m in range(GROUP // 2):
        a, b = o[2 * m * tq:(2 * m + 1) * tq], o[(2 * m + 1) * tq:(2 * m + 2) * tq]
        pair = jnp.where(lo, a, pltpu.roll(b, HEAD_DIM, 1)) if g == 0 else jnp.where(lo, pltpu.roll(a, HEAD_DIM, 1), b)
        o_ref[0, :, LANES * (first_pair + m):LANES * (first_pair + m + 1)] = pair


def _fox_p_kernel(q_ref, k_ref, v_ref, c_ref, ct_ref, o_ref, *, tq, tk, nc):
    i = pl.program_id(1)
    nfull = (i * tq) // tk
    for g in range(FOX_KV_HEADS):
        heads = range(GROUP * g, GROUP * (g + 1))
        qg = _rows([(q_ref[0, :, LANES * h:LANES * (h + 1)] * FOX_SCALE).astype(BF16) for h in heads])
        cq = _rows([c_ref[0, :, h:h + 1] for h in heads])

        def chunk(j, st, masked, qg=qg, cq=cq, heads=heads):
            kc = k_ref[0, pl.ds(j * tk, tk), :].astype(BF16)
            vc = v_ref[0, pl.ds(j * tk, tk), :].astype(BF16)
            ck = _rows([jnp.broadcast_to(ct_ref[0, pl.ds(h * nc + j, 1), :], (tq, tk)) for h in heads])
            s = _dot_nt(qg, kc) + (cq - ck)
            if masked:
                s = _causal_rows(s, i, j, tq, tk)
            return _online(st, s, lambda p: _dot(p, vc))

        st = lax.fori_loop(0, nfull, functools.partial(chunk, masked=False), _init_state(GROUP * tq, LANES))
        _, l_i, acc = chunk(nfull, st, True)
        _compact_pairs(acc / l_i, g, tq, o_ref, g * (GROUP // 2))


def _fox_prompt(h3, c, ct, tq, tk):
    b, t, _ = h3.shape
    nc = t // tk
    ctr = ct.reshape(b, FOX_HEADS * nc, tk)
    return pl.pallas_call(
        functools.partial(_fox_p_kernel, tq=tq, tk=tk, nc=nc),
        grid=(b, t // tq),
        in_specs=[pl.BlockSpec((1, tq, N_HEADS * LANES), lambda bi, i: (bi, i, _blk('fox_q'))),
                  pl.BlockSpec((1, t, LANES), lambda bi, i: (bi, 0, _blk('fox_k'))),
                  pl.BlockSpec((1, t, LANES), lambda bi, i: (bi, 0, _blk('fox_v'))),
                  pl.BlockSpec((1, tq, FOX_HEADS), lambda bi, i: (bi, i, 0)),
                  pl.BlockSpec((1, FOX_HEADS * nc, tk), lambda bi, i: (bi, 0, 0))],
        out_specs=pl.BlockSpec((1, tq, FOX_W), lambda bi, i: (bi, i, 0)),
        out_shape=jax.ShapeDtypeStruct((b, t, FOX_W), F32),
        compiler_params=_cparams(2),
        name='fox_prompt',
    )(h3, h3, h3, c, ctr)


def _mla_p_kernel(ql_ref, qr_ref, ckv_ref, kr_ref, wv_ref, o_ref, *, tq, tk):
    i = pl.program_id(1)
    nfull = (i * tq) // tk
    ql = _rows([ql_ref[0, :, MLA_KV_LORA * h:MLA_KV_LORA * (h + 1)].astype(BF16) for h in range(MLA_HEADS)])
    qr = _rows([qr_ref[0, :, LANES * h:LANES * (h + 1)].astype(BF16) for h in range(MLA_HEADS)])

    def chunk(j, st, masked):
        kc = ckv_ref[0, pl.ds(j * tk, tk), :].astype(BF16)
        kr = kr_ref[0, pl.ds(j * tk, tk), :].astype(BF16)
        s = _dot_nt(ql, kc) + _dot_nt(qr, kr)
        if masked:
            s = _causal_rows(s, i, j, tq, tk)
        return _online(st, s, lambda p: _dot(p, kc))

    st = lax.fori_loop(0, nfull, functools.partial(chunk, masked=False), _init_state(MLA_HEADS * tq, MLA_KV_LORA))
    _, l_i, acc = chunk(nfull, st, True)
    o = (acc / l_i).astype(BF16)
    for m in range(MLA_HEADS // 2):
        o_ref[0, :, LANES * m:LANES * (m + 1)] = (_dot(o[2 * m * tq:(2 * m + 1) * tq], wv_ref[2 * m])
                                                  + _dot(o[(2 * m + 1) * tq:(2 * m + 2) * tq], wv_ref[2 * m + 1]))


def _mla_prompt(qlat, qrope, ckv, kr, wv_slot, tq, tk):
    b, t, _ = qlat.shape
    return pl.pallas_call(
        functools.partial(_mla_p_kernel, tq=tq, tk=tk),
        grid=(b, t // tq),
        in_specs=[pl.BlockSpec((1, tq, qlat.shape[2]), lambda bi, i: (bi, i, 0)),
                  pl.BlockSpec((1, tq, qrope.shape[2]), lambda bi, i: (bi, i, 0)),
                  pl.BlockSpec((1, t, MLA_KV_LORA), lambda bi, i: (bi, 0, 0)),
                  pl.BlockSpec((1, t, LANES), lambda bi, i: (bi, 0, 0)),
                  pl.BlockSpec(wv_slot.shape, lambda bi, i: (0, 0, 0))],
        out_specs=pl.BlockSpec((1, tq, MLA_W), lambda bi, i: (bi, i, 0)),
        out_shape=jax.ShapeDtypeStruct((b, t, MLA_W), F32),
        compiler_params=_cparams(2),
        name='mla_prompt',
    )(qlat, qrope, ckv, kr, wv_slot)


def _sort_key(score):
    b = lax.bitcast_convert_type(jnp.where(score == 0.0, 0.0, score), I32)
    return b ^ ((b >> 31) & 0x7FFFFFFF)


def _select_bias(keys_ref, write, topk, n_used):
    nc, rows, tk = keys_ref.shape
    kf = float(topk)
    rb = rows

    def counts(preds):
        out = []
        for r0 in range(0, rows, rb):
            def body(j, accs, r0=r0):
                x = keys_ref[j, r0:r0 + rb, :]
                new = []
                for acc, pred in zip(accs, preds):
                    w = jnp.where(pred(x, j, r0), 1.0, 0.0)
                    new.append(acc + sum(w[:, LANES * c:LANES * (c + 1)] for c in range(tk // LANES)))
                return tuple(new)
            accs = lax.fori_loop(0, n_used, body, tuple(jnp.zeros((rb, LANES), F32) for _ in preds))
            out.append([jnp.sum(a, axis=-1, keepdims=True) for a in accs])
        return [_rows([o[n] for o in out]) for n in range(len(preds))]

    def count(pred):
        return counts([pred])[0]

    def ge(cand):
        return lambda x, j, r0: x >= cand[r0:r0 + rb]

    zero = jnp.zeros((rows, 1), I32)
    if rows <= SUBLANES:
        def refine(cur, c1, c2, c3):
            n1, n2, n3 = counts([ge(c1), ge(c2), ge(c3)])
            return jnp.where(n3 >= kf, c3, jnp.where(n2 >= kf, c2, jnp.where(n1 >= kf, c1, cur)))

        low = jnp.full((rows, 1), INT_MIN, I32)
        b30 = jnp.int32(1 << 30)
        cur = refine(low, low | b30, zero, zero | b30)

        def vbody(b, cur):
            hi = jnp.left_shift(jnp.int32(1), 29 - 2 * b)
            lo = jnp.left_shift(jnp.int32(1), 28 - 2 * b)
            return refine(cur, cur | lo, cur | hi, cur | hi | lo)

        thr = lax.fori_loop(0, 15, vbody, cur)
    else:
        cur = jnp.where(count(ge(zero)) >= kf, 0, INT_MIN).astype(I32)

        def vbody(b, cur):
            cand = cur | jnp.left_shift(jnp.int32(1), 30 - b)
            return jnp.where(count(ge(cand)) >= kf, cand, cur)

        thr = lax.fori_loop(0, 31, vbody, cur)
    eq = lambda x, j, r0: x == thr[r0:r0 + rb]
    n_gt, n_eq = counts([lambda x, j, r0: x > thr[r0:r0 + rb], eq])
    need = kf - n_gt
    lane = _iota((rb, tk), 1)
    nbits = max(1, int(math.ceil(math.log2(nc * tk))))

    def tie_break():
        def ibody(b, p):
            cand = p + jnp.left_shift(jnp.int32(1), nbits - 1 - b)
            n_lt = count(lambda x, j, r0: (x == thr[r0:r0 + rb]) & (j * tk + lane < cand[r0:r0 + rb]))
            return jnp.where(n_lt < need, cand, p)
        return lax.fori_loop(0, nbits, ibody, jnp.zeros((rows, 1), I32))

    excess = jnp.max(jnp.where(thr > INT_MIN, n_eq - need, 0.0))
    p = lax.cond(excess > 0.0, tie_break, lambda: jnp.full((rows, 1), nc * tk, I32))
    lane_all = _iota((rows, tk), 1)

    def wbody(j, carry):
        x = keys_ref[j]
        sel = (x > INT_MIN) & ((x > thr) | ((x == thr) & (j * tk + lane_all <= p)))
        write(j, jnp.where(sel, 0.0, NEG))
        return carry

    lax.fori_loop(0, n_used, wbody, 0)


def _dsa_p_kernel(iq_ref, iw_ref, ik_ref, q_ref, k_ref, v_ref, o_ref, keys_ref, bias_ref, *, tq, tk, topk):
    i = pl.program_id(1)
    n_used = (i * tq) // tk + 1
    iq = _rows([iq_ref[0, :, LANES * h:LANES * (h + 1)].astype(BF16) for h in range(IDX_HEADS)])
    iw = _rows([iw_ref[0, :, h:h + 1] for h in range(IDX_HEADS)])

    def ibody(j, carry):
        w = jnp.maximum(_dot_nt(iq, ik_ref[0, pl.ds(j * tk, tk), :].astype(BF16)), 0.0) * iw
        score = w[0:tq]
        for h in range(1, IDX_HEADS):
            score = score + w[h * tq:(h + 1) * tq]
        qpos = i * tq + _iota((tq, tk), 0)
        kpos = j * tk + _iota((tq, tk), 1)
        keys_ref[j] = jnp.where(kpos <= qpos, _sort_key(score), INT_MIN)
        return carry

    lax.fori_loop(0, n_used, ibody, 0)

    def write(j, bias):
        bias_ref[j] = bias

    _select_bias(keys_ref, write, topk, n_used)

    for g in range(DSA_KV_HEADS):
        qg = _rows([q_ref[0, :, LANES * h:LANES * (h + 1)].astype(BF16) for h in range(GROUP * g, GROUP * (g + 1))])

        def chunk(j, st, qg=qg):
            kc = k_ref[0, pl.ds(j * tk, tk), :].astype(BF16)
            vc = v_ref[0, pl.ds(j * tk, tk), :].astype(BF16)
            bias = bias_ref[j]
            return _online(st, _dot_nt(qg, kc) + _rows([bias] * GROUP), lambda p: _dot(p, vc))

        _, l_i, acc = lax.fori_loop(0, n_used, chunk, _init_state(GROUP * tq, LANES))
        _compact_pairs(acc / l_i, g, tq, o_ref, g * (GROUP // 2))


def _dsa_prompt(iq, iw, ik, dq, dk, h3, tq, tk):
    b, t, _ = dq.shape
    nc = t // tk
    topk = min(TOPK_MAX, t // 4)
    return pl.pallas_call(
        functools.partial(_dsa_p_kernel, tq=tq, tk=tk, topk=topk),
        grid=(b, t // tq),
        in_specs=[pl.BlockSpec((1, tq, N_HEADS * LANES), lambda bi, i: (bi, i, 0)),
                  pl.BlockSpec((1, tq, IDX_HEADS), lambda bi, i: (bi, i, 0)),
                  pl.BlockSpec((1, t, LANES), lambda bi, i: (bi, 0, 0)),
                  pl.BlockSpec((1, tq, N_HEADS * LANES), lambda bi, i: (bi, i, 0)),
                  pl.BlockSpec((1, t, LANES), lambda bi, i: (bi, 0, 0)),
                  pl.BlockSpec((1, t, LANES), lambda bi, i: (bi, 0, _blk('dsa_v')))],
        out_specs=pl.BlockSpec((1, tq, DSA_W), lambda bi, i: (bi, i, 0)),
        out_shape=jax.ShapeDtypeStruct((b, t, DSA_W), F32),
        scratch_shapes=[pltpu.VMEM((nc, tq, tk), I32), pltpu.VMEM((nc, tq, tk), F32)],
        compiler_params=_cparams(2),
        name='dsa_prompt',
    )(iq, iw, ik, dq, dk, h3)


def _mem_kernel(q_ref, k_ref, v_ref, o_ref):
    tq = q_ref.shape[1]
    half = _iota((k_ref.shape[-2], LANES), 1) // HEAD_DIM
    for m in range(MEM_HEADS // 2):
        qp = q_ref[0, :, LANES * m:LANES * (m + 1)].astype(BF16)
        kp = k_ref[:, LANES * m:LANES * (m + 1)] if k_ref.ndim == 2 else k_ref[0, :, LANES * m:LANES * (m + 1)]
        vp = v_ref[:, LANES * m:LANES * (m + 1)] if v_ref.ndim == 2 else v_ref[0, :, LANES * m:LANES * (m + 1)]
        pair = jnp.zeros((tq, LANES), F32)
        for s in range(2):
            kc = jnp.where(half == s, kp, 0.0).astype(BF16)
            vc = jnp.where(half == s, vp, 0.0).astype(BF16)
            sc = _dot_nt(qp, kc) * MEM_SCALE
            p = jnp.exp(sc - jnp.max(sc, axis=-1, keepdims=True))
            p = p / jnp.sum(p, axis=-1, keepdims=True)
            pair = pair + _dot(p.astype(BF16), vc)
        o_ref[0, :, LANES * m:LANES * (m + 1)] = pair


def _mem_attention(q_arr, q_spec, k_arr, k_spec, v_arr, v_spec, grid, out_shape, out_spec):
    return pl.pallas_call(
        _mem_kernel, grid=grid, in_specs=[q_spec, k_spec, v_spec], out_specs=out_spec,
        out_shape=jax.ShapeDtypeStruct(out_shape, F32), compiler_params=_cparams(len(grid)), name='mem_attention',
    )(q_arr, k_arr, v_arr)


def _finish_kernel(x_ref, fo_ref, mo_ref, do_ref, eo_ref, fz_ref, mz_ref, dz_ref, ez_ref, gt_ref,
                   wb_ref, wo_ref, g_ref, b_ref, o_ref, *, alpha):
    offs = np.cumsum((0,) + BRANCH_WIDTHS).tolist()
    d = x_ref.shape[-1]
    y = jnp.zeros(x_ref.shape, F32)
    for n, (o_r, z_r) in enumerate(((fo_ref, fz_ref), (mo_ref, mz_ref), (do_ref, dz_ref), (eo_ref, ez_ref))):
        z = z_r[...]
        u = o_r[...] * (z * jax.nn.sigmoid(z))
        pb = _dot(u.astype(BF16), wb_ref[offs[n]:offs[n + 1], :])
        y = y + jax.nn.sigmoid(gt_ref[:, d * n:d * (n + 1)]) * pb
    r = alpha * x_ref[...] + _dot(y.astype(BF16), wo_ref[...])
    mu = jnp.mean(r, axis=-1, keepdims=True)
    var = jnp.mean(jnp.square(r - mu), axis=-1, keepdims=True)
    o_ref[...] = (r - mu) * lax.rsqrt(var + LN_EPS) * g_ref[...] + b_ref[...]


def _finish(x2, h2, fo, mo, do, eo, wb, wo, g, b, tr, alpha):
    n, d = x2.shape
    row = lambda w: pl.BlockSpec((tr, w), lambda i: (i, 0))
    hcol = lambda name: pl.BlockSpec((tr, _LAYOUT[name][1]), lambda i, name=name: (i, _blk(name)))
    full = lambda a: pl.BlockSpec(a.shape, lambda i: (0,) * a.ndim)
    return pl.pallas_call(
        functools.partial(_finish_kernel, alpha=alpha),
        grid=(n // tr,),
        in_specs=[row(d), row(FOX_W), row(MLA_W), row(DSA_W), row(MEM_W),
                  hcol('fox_z'), hcol('mla_z'), hcol('dsa_z'), hcol('mem_z'), hcol('gates'),
                  full(wb), full(wo), full(g), full(b)],
        out_specs=row(d),
        out_shape=jax.ShapeDtypeStruct((n, d), F32),
        compiler_params=_cparams(1),
        name='finish',
    )(x2, fo, mo, do, eo, h2, h2, h2, h2, h2, wb, wo, g, b)


def _paged_fetch(pt_ref, srcs, bufs, sem, layer, pg, page_of):
    b, p = pl.program_id(0), pl.program_id(1)
    nb, steps = pl.num_programs(0), pl.num_programs(1)
    step = b * steps + p
    slot = step % 2

    def copies(bb, pp, sl):
        return [pltpu.make_async_copy(src.at[pt_ref[bb, page_of(pp, j)], layer], buf.at[sl, j], sem.at[n, sl])
                for n, (src, buf) in enumerate(zip(srcs, bufs)) for j in range(pg)]

    @pl.when(step == 0)
    def _():
        for c in copies(b, p, slot):
            c.start()

    @pl.when(step + 1 < nb * steps)
    def _():
        wrap = p + 1 == steps
        for c in copies(jnp.where(wrap, b + 1, b), jnp.where(wrap, 0, p + 1), 1 - slot):
            c.start()

    for c in copies(b, p, slot):
        c.wait()
    return [[buf.at[slot, j] for j in range(pg)] for buf in bufs]


def _page_scratch(caches, pg):
    return ([pltpu.VMEM((2, pg) + tuple(c.shape[2:]), c.dtype) for c in caches]
            + [pltpu.SemaphoreType.DMA((len(caches), 2))])


_HBM = pl.BlockSpec(memory_space=pl.ANY)


def _per_b(a):
    return pl.BlockSpec((1,) + a.shape[1:], lambda b, p, pt: (b,) + (0,) * (a.ndim - 1))


def _scratch_init(m_s, l_s, acc_s):
    m_s[...] = jnp.full(m_s.shape, NEG, F32)
    l_s[...] = jnp.zeros(l_s.shape, F32)
    acc_s[...] = jnp.zeros(acc_s.shape, F32)


def _scratch_update(m_s, l_s, acc_s, s, pv):
    m_new, l_new, acc_new = _online((m_s[...], l_s[...], acc_s[...]), s, pv)
    m_s[...], l_s[...], acc_s[...] = m_new, l_new, acc_new


def _chain_updates(m_s, l_s, acc_s, logit_pages, pv_of, pages):
    n = m_s.shape[0]
    per = len(pages) // n
    for c in range(n):
        sl = slice(c * per, (c + 1) * per)
        _scratch_update(m_s.at[c], l_s.at[c], acc_s.at[c], jnp.concatenate(logit_pages[sl], axis=1), pv_of(pages[sl]))


def _chain_merge(m_s, l_s, acc_s):
    m = m_s[0]
    for c in range(1, m_s.shape[0]):
        m = jnp.maximum(m, m_s[c])
    num = sum(jnp.exp(m_s[c] - m) * acc_s[c] for c in range(m_s.shape[0]))
    den = sum(jnp.exp(m_s[c] - m) * l_s[c] for c in range(m_s.shape[0]))
    return num / den


def _chain_scratch(rows, dv, pg):
    n = 2 if pg % 2 == 0 else 1
    return [pltpu.VMEM((n, rows, 1), F32), pltpu.VMEM((n, rows, 1), F32), pltpu.VMEM((n, rows, dv), F32)]


def _paged_pv(vts):
    def pv(p):
        return sum(_dot_nt(p[:, PAGE_SIZE * j:PAGE_SIZE * (j + 1)], vts[j][...].astype(BF16)) for j in range(len(vts)))
    return pv


def _fox_s_kernel(pt_ref, q_ref, lrows_ref, kn_ref, vn_ref, lfn_ref, ck_ref, cv_ref, clf_ref, o_ref,
                  m_s, l_s, acc_s, carry_s, kbuf, vbuf, lbuf, sem, *, pg, n_tok, layer, page_of):
    kps, vps, lps = _paged_fetch(pt_ref, (ck_ref, cv_ref, clf_ref), (kbuf, vbuf, lbuf), sem, layer, pg, page_of)
    p = pl.program_id(1)
    rows = q_ref.shape[1]
    q = (q_ref[0] * FOX_SCALE).astype(BF16)
    lane = _iota((rows, PAGE_SIZE), 1)
    t_row = _iota((rows, PAGE_SIZE), 0) // N_HEADS
    sp = _iota((PAGE_SIZE, 2 * PAGE_SIZE), 0)
    cc = _iota((PAGE_SIZE, 2 * PAGE_SIZE), 1)
    msk = jnp.where(((cc < PAGE_SIZE) & (sp >= cc)) | ((cc >= PAGE_SIZE) & (sp > cc - PAGE_SIZE)), 1.0, 0.0).astype(BF16)
    rowc = jnp.sum(jnp.where(lane > t_row, lrows_ref[0], 0.0), axis=-1, keepdims=True)

    def logits(kts, lfts, carry):
        out = []
        nh = lfts[0].shape[0]
        sfx_all = sum(_dot(part, msk) for part in _split3(_rows(lfts)))
        for n, kt in enumerate(kts):
            sfx = sfx_all[nh * n:nh * (n + 1)]
            bias8 = sfx[:, PAGE_SIZE:] + carry
            carry = carry + sfx[:, 0:1]
            out.append(_dot(q, kt.astype(BF16)) + (_rows([bias8] * n_tok) - rowc))
        return out, carry

    @pl.when(p == 0)
    def _():
        _scratch_init(m_s, l_s, acc_s)
        (s,), carry = logits([kn_ref[0]], [lfn_ref[0]], jnp.zeros(carry_s.shape, F32))
        carry_s[...] = carry
        _scratch_update(m_s.at[0], l_s.at[0], acc_s.at[0], jnp.where(lane <= t_row, s, NEG), _paged_pv([vn_ref.at[0]]))

    ss, carry = logits([r[...] for r in kps], [r[...] for r in lps], carry_s[...])
    carry_s[...] = carry
    _chain_updates(m_s, l_s, acc_s, ss, _paged_pv, vps)

    @pl.when(p == pl.num_programs(1) - 1)
    def _():
        o_ref[0] = _chain_merge(m_s, l_s, acc_s)


def _fox_sample(pt, q, lrows, knt, vnt, lfnt, ckt, cvt, clft, layer, pg):
    bs, npages = pt.shape
    rows = q.shape[1]
    rev = lambda p, j: npages - 1 - (p * pg + j)
    grid_spec = pltpu.PrefetchScalarGridSpec(
        num_scalar_prefetch=1, grid=(bs, npages // pg),
        in_specs=[_per_b(q), _per_b(lrows), _per_b(knt), _per_b(vnt), _per_b(lfnt), _HBM, _HBM, _HBM],
        out_specs=pl.BlockSpec((1, rows, LANES), lambda b, p, pt: (b, 0, 0)),
        scratch_shapes=_chain_scratch(rows, LANES, pg) + [pltpu.VMEM((FOX_HEADS, 1), F32)]
        + _page_scratch((ckt, cvt, clft), pg))
    return pl.pallas_call(
        functools.partial(_fox_s_kernel, pg=pg, n_tok=rows // N_HEADS, layer=layer, page_of=rev),
        grid_spec=grid_spec, out_shape=jax.ShapeDtypeStruct((bs, rows, LANES), F32),
        compiler_params=_cparams(2), name='fox_sample',
    )(pt, q, lrows, knt, vnt, lfnt, ckt, cvt, clft)


def _mla_s_kernel(pt_ref, ql_ref, qr_ref, cn_ref, rn_ref, cc_ref, cr_ref, o_ref, m_s, l_s, acc_s, cbuf, rbuf, sem,
                  *, pg, layer, page_of):
    cps, rps = _paged_fetch(pt_ref, (cc_ref, cr_ref), (cbuf, rbuf), sem, layer, pg, page_of)
    p = pl.program_id(1)
    rows = ql_ref.shape[1]
    ql = ql_ref[0].astype(BF16)
    qr = qr_ref[0].astype(BF16)
    lane = _iota((rows, PAGE_SIZE), 1)
    t_row = _iota((rows, PAGE_SIZE), 0) // N_HEADS

    def logits(c, rt):
        return _dot_nt(ql, c.astype(BF16)) + _dot(qr, rt.astype(BF16))

    def pv_of(cs):
        def pv(pr):
            return sum(_dot(pr[:, PAGE_SIZE * j:PAGE_SIZE * (j + 1)], cs[j][...].astype(BF16)) for j in range(len(cs)))
        return pv

    @pl.when(p == 0)
    def _():
        _scratch_init(m_s, l_s, acc_s)
        s = logits(cn_ref[0], rn_ref[0])
        _scratch_update(m_s.at[0], l_s.at[0], acc_s.at[0], jnp.where(lane <= t_row, s, NEG), pv_of([cn_ref.at[0]]))

    _chain_updates(m_s, l_s, acc_s, [logits(cps[j][...], rps[j][...]) for j in range(pg)], pv_of, cps)

    @pl.when(p == pl.num_programs(1) - 1)
    def _():
        o_ref[0] = _chain_merge(m_s, l_s, acc_s)


def _mla_sample(pt, ql, qr, cn, rnt, cckv, ckrt, layer, pg):
    bs, npages = pt.shape
    rows = ql.shape[1]
    fwd = lambda p, j: p * pg + j
    grid_spec = pltpu.PrefetchScalarGridSpec(
        num_scalar_prefetch=1, grid=(bs, npages // pg),
        in_specs=[_per_b(ql), _per_b(qr), _per_b(cn), _per_b(rnt), _HBM, _HBM],
        out_specs=pl.BlockSpec((1, rows, MLA_KV_LORA), lambda b, p, pt: (b, 0, 0)),
        scratch_shapes=_chain_scratch(rows, MLA_KV_LORA, pg) + _page_scratch((cckv, ckrt), pg))
    return pl.pallas_call(
        functools.partial(_mla_s_kernel, pg=pg, layer=layer, page_of=fwd),
        grid_spec=grid_spec, out_shape=jax.ShapeDtypeStruct((bs, rows, MLA_KV_LORA), F32),
        compiler_params=_cparams(2), name='mla_sample',
    )(pt, ql, qr, cn, rnt, cckv, ckrt)


def _mla_up_kernel(o_ref, wv_ref, out_ref):
    full = _dot(o_ref[0].astype(BF16), wv_ref[...])
    head = _iota(full.shape, 0) % MLA_HEADS
    keep = (_iota(full.shape, 1) // MLA_V) == head
    out_ref[0] = jnp.where(keep, full, 0.0)


def _mla_up(olat, wv_all):
    bs, rows, _ = olat.shape
    return pl.pallas_call(
        _mla_up_kernel, grid=(bs,),
        in_specs=[pl.BlockSpec((1, rows, MLA_KV_LORA), lambda b: (b, 0, 0)), pl.BlockSpec(wv_all.shape, lambda b: (0, 0))],
        out_specs=pl.BlockSpec((1, rows, MLA_W), lambda b: (b, 0, 0)),
        out_shape=jax.ShapeDtypeStruct((bs, rows, MLA_W), F32), compiler_params=_cparams(1), name='mla_up_sample',
    )(olat, wv_all)


def _dsa_idx_s_kernel(pt_ref, iq_ref, iw_ref, ikn_ref, ci_ref, bias_ref, keys_ref, ibuf, sem,
                      *, pg, topk, n_tok, layer, page_of):
    (ips,) = _paged_fetch(pt_ref, (ci_ref,), (ibuf,), sem, layer, pg, page_of)
    p = pl.program_id(1)
    rows = iq_ref.shape[1]
    steps = keys_ref.shape[0] - 1
    trows = keys_ref.shape[1]
    iq = iq_ref[0].astype(BF16)
    iw = iw_ref[0]
    tok = _iota((trows, PAGE_SIZE), 0)
    lane = _iota((trows, PAGE_SIZE), 1)
    pad_rows = jnp.zeros((trows - n_tok, PAGE_SIZE), F32)

    def score_keys(ikt):
        w = jnp.maximum(_dot(iq, ikt.astype(BF16)), 0.0) * iw
        tot = jnp.sum(w.reshape(n_tok, IDX_HEADS, PAGE_SIZE), axis=1)
        return jnp.where(tok < n_tok, _sort_key(jnp.concatenate([tot, pad_rows], axis=0)), INT_MIN)

    @pl.when(p == 0)
    def _():
        keys_ref[steps] = jnp.full(keys_ref.shape[1:], INT_MIN, I32)
        keys_ref[steps, :, 0:PAGE_SIZE] = jnp.where(lane <= tok, score_keys(ikn_ref[0]), INT_MIN)

    for j in range(pg):
        keys_ref[p, :, PAGE_SIZE * j:PAGE_SIZE * (j + 1)] = score_keys(ips[j][...])

    def write(j, bias):
        bias_ref[0, j] = _rows([jnp.broadcast_to(bias[t:t + 1, :], (IDX_HEADS, bias.shape[1])) for t in range(n_tok)])

    @pl.when(p == steps - 1)
    def _():
        _select_bias(keys_ref, write, topk, steps + 1)


def _dsa_idx_sample(pt, iq, iw, iknt, cidxt, layer, pg, topk):
    bs, npages = pt.shape
    rows = iq.shape[1]
    fwd = lambda p, j: p * pg + j
    steps = npages // pg
    grid_spec = pltpu.PrefetchScalarGridSpec(
        num_scalar_prefetch=1, grid=(bs, steps),
        in_specs=[_per_b(iq), _per_b(iw), _per_b(iknt), _HBM],
        out_specs=pl.BlockSpec((1, steps + 1, rows, pg * PAGE_SIZE), lambda b, p, pt: (b, 0, 0, 0)),
        scratch_shapes=[pltpu.VMEM((steps + 1, SUBLANES, pg * PAGE_SIZE), I32)] + _page_scratch((cidxt,), pg))
    assert rows // IDX_HEADS <= SUBLANES
    return pl.pallas_call(
        functools.partial(_dsa_idx_s_kernel, pg=pg, topk=topk, n_tok=rows // IDX_HEADS, layer=layer, page_of=fwd),
        grid_spec=grid_spec, out_shape=jax.ShapeDtypeStruct((bs, steps + 1, rows, pg * PAGE_SIZE), F32),
        compiler_params=_cparams(2), name='dsa_index_sample',
    )(pt, iq, iw, iknt, cidxt)


def _dsa_s_kernel(pt_ref, q_ref, kn_ref, vn_ref, bn_ref, bp_ref, ck_ref, cv_ref, o_ref, m_s, l_s, acc_s, kbuf, vbuf, sem,
                  *, pg, layer, page_of):
    kps, vps = _paged_fetch(pt_ref, (ck_ref, cv_ref), (kbuf, vbuf), sem, layer, pg, page_of)
    p = pl.program_id(1)
    q = q_ref[0].astype(BF16)

    @pl.when(p == 0)
    def _():
        _scratch_init(m_s, l_s, acc_s)
        _scratch_update(m_s.at[0], l_s.at[0], acc_s.at[0],
                        _dot(q, kn_ref[0].astype(BF16)) + bn_ref[0, :, 0:PAGE_SIZE], _paged_pv([vn_ref.at[0]]))

    _chain_updates(m_s, l_s, acc_s, [_dot(q, kps[j][...].astype(BF16)) + bp_ref[0, :, PAGE_SIZE * j:PAGE_SIZE * (j + 1)]
                                     for j in range(pg)], _paged_pv, vps)

    @pl.when(p == pl.num_programs(1) - 1)
    def _():
        o_ref[0] = _chain_merge(m_s, l_s, acc_s)


def _dsa_sample(pt, q, knt, vnt, bias, ckt, cvt, layer, pg):
    bs, npages = pt.shape
    rows = q.shape[1]
    fwd = lambda p, j: p * pg + j
    grid_spec = pltpu.PrefetchScalarGridSpec(
        num_scalar_prefetch=1, grid=(bs, npages // pg),
        in_specs=[_per_b(q), _per_b(knt), _per_b(vnt),
                  pl.BlockSpec((None, 1, rows, pg * PAGE_SIZE), lambda b, p, pt: (b, npages // pg, 0, 0)),
                  pl.BlockSpec((None, 1, rows, pg * PAGE_SIZE), lambda b, p, pt: (b, p, 0, 0)), _HBM, _HBM],
        out_specs=pl.BlockSpec((1, rows, LANES), lambda b, p, pt: (b, 0, 0)),
        scratch_shapes=_chain_scratch(rows, LANES, pg) + _page_scratch((ckt, cvt), pg))
    return pl.pallas_call(
        functools.partial(_dsa_s_kernel, pg=pg, layer=layer, page_of=fwd),
        grid_spec=grid_spec, out_shape=jax.ShapeDtypeStruct((bs, rows, LANES), F32),
        compiler_params=_cparams(2), name='dsa_sample',
    )(pt, q, knt, vnt, bias, bias, ckt, cvt)


def _pad_cols(w):
    zeros = lambda n: jnp.zeros(w.shape[:2] + (n,), w.dtype)
    parts = []
    for name, pad, slotting in _GROUPS:
        if name == 'pad':
            parts.append(zeros(pad))
            continue
        src, real = _SRC[name], _WID[name]
        if slotting is None:
            parts.append(w[:, :, src:src + real])
            if pad > real:
                parts.append(zeros(pad - real))
            continue
        for h in range(N_HEADS):
            head = w[:, :, src + HEAD_DIM * h:src + HEAD_DIM * (h + 1)]
            first = slotting == 'lo' or h // GROUP == 0
            parts.extend([head, zeros(HEAD_DIM)] if first else [zeros(HEAD_DIM), head])
    return jnp.concatenate(parts, axis=-1)


def _rot_tables(pos, half, width, reps):
    inv_freq = jnp.exp(-math.log(ROPE_THETA) * jnp.arange(half, dtype=F32) / half)
    ang = pos.astype(F32)[:, None] * inv_freq[None, :]
    cos, sin = jnp.cos(ang), jnp.sin(ang)
    n = pos.shape[0]
    cos_h = jnp.concatenate([cos, cos, jnp.ones((n, width - 2 * half), F32)], axis=-1)
    sin_h = jnp.concatenate([-sin, sin, jnp.zeros((n, width - 2 * half), F32)], axis=-1)
    return jnp.tile(cos_h, (1, reps)), jnp.tile(sin_h, (1, reps))


def _new_page_t(a):
    return jnp.pad(jnp.swapaxes(a, 1, 2), ((0, 0), (0, 0), (0, PAGE_SIZE - a.shape[1])))


def kernel(x_prompt, x_sample, cache_fox_k, cache_fox_v, cache_fox_logf, cache_mla_ckv, cache_mla_krope,
           cache_dsa_k, cache_dsa_v, cache_dsa_kidx, cache_mem_k, cache_mem_v, page_table, mem_prompt,
           w_in, b_forget, mla_q_norm, mla_kv_norm, w_mla_q_up, w_mla_kv_up, w_mem_kv, w_branch, w_out,
           ln_gain, ln_bias):
    b, t, d = x_prompt.shape
    bs, ts, _ = x_sample.shape
    depth = w_in.shape[0]
    mem_len = mem_prompt.shape[1]
    npages = page_table.shape[1]
    past_len = npages * PAGE_SIZE
    pool = cache_fox_k.shape[0]
    rows = ts * N_HEADS
    alpha = (2 * depth) ** 0.25
    assert FOX_HEADS == MLA_HEADS == DSA_HEADS == IDX_HEADS == N_HEADS and ts <= PAGE_SIZE

    tq = min(128, t)
    tq_dsa = min(256, t)
    tk = min(512, t)
    tk_mla = min(512, t)
    tq_fox = min(256, t)
    tr = min(256, t)
    pg = math.gcd(npages, 64)
    n_s = bs * ts
    topk_s = min(TOPK_MAX, (past_len + ts) // 4)

    w_pad = _pad_cols(w_in).astype(BF16)
    bf_pad = jnp.pad(b_forget, ((0, 0), (0, LANES - FOX_HEADS)))[:, None, :]
    qn_pad = jnp.pad(mla_q_norm, ((0, 0), (0, 512 - MLA_Q_LORA)))[:, None, :]
    kvn = mla_kv_norm[:, None, :]
    wq = w_mla_q_up.reshape(depth, MLA_Q_LORA, MLA_HEADS, MLA_NOPE + MLA_ROPE)
    wq_pad = lambda a: jnp.pad(a, ((0, 0), (0, 512 - MLA_Q_LORA), (0, 0), (0, LANES - a.shape[-1]))
                               ).reshape(depth, 512, MLA_HEADS * LANES).astype(BF16)
    wq_nope, wq_rope = wq_pad(wq[..., :MLA_NOPE]), wq_pad(wq[..., MLA_NOPE:])
    wkt = jnp.pad(jnp.transpose(w_mla_kv_up[..., :MLA_NOPE], (0, 2, 3, 1)), ((0, 0), (0, 0), (0, LANES - MLA_NOPE), (0, 0))
                  ).reshape(depth, MLA_HEADS * LANES, MLA_KV_LORA).astype(BF16)
    wv = jnp.transpose(w_mla_kv_up[..., MLA_NOPE:], (0, 2, 1, 3))
    wv_slot = jnp.stack([jnp.pad(wv[:, h], ((0, 0), (0, 0), ((h % 2) * MLA_V, (1 - h % 2) * MLA_V)))
                         for h in range(MLA_HEADS)], axis=1).astype(BF16)
    wv_all = jnp.transpose(wv, (0, 2, 1, 3)).reshape(depth, MLA_KV_LORA, MLA_W).astype(BF16)
    w_mem = w_mem_kv.astype(BF16)
    wb, wo = w_branch.astype(BF16), w_out.astype(BF16)
    ln_g, ln_b = ln_gain[:, None, :], ln_bias[:, None, :]

    pos_p = jnp.arange(t, dtype=jnp.int32)
    pos_s = jnp.tile(past_len + jnp.arange(ts, dtype=jnp.int32), bs)
    tabs_p = _rot_tables(pos_p, ROT_DIM // 2, HEAD_DIM, 2) + _rot_tables(pos_p, MLA_ROPE // 2, MLA_ROPE, 4)
    tabs_s = _rot_tables(pos_s, ROT_DIM // 2, HEAD_DIM, 2) + _rot_tables(pos_s, MLA_ROPE // 2, MLA_ROPE, 4)

    kv_t = lambda c: jnp.transpose(c, (0, 1, 3, 4, 2)).reshape(pool, depth, FOX_KV_HEADS * HEAD_DIM, PAGE_SIZE)
    c_fkt, c_fvt, c_dkt, c_dvt = kv_t(cache_fox_k), kv_t(cache_fox_v), kv_t(cache_dsa_k), kv_t(cache_dsa_v)
    c_lft = jnp.swapaxes(cache_fox_logf, 2, 3)
    c_krt = jnp.swapaxes(cache_mla_krope, 2, 3)
    c_ixt = jnp.swapaxes(cache_dsa_kidx, 2, 3)
    c_mk = cache_mem_k.reshape(bs, depth, mem_len, MEM_W)
    c_mv = cache_mem_v.reshape(bs, depth, mem_len, MEM_W)

    xp = x_prompt.reshape(b * t, d)
    xs = x_sample.reshape(n_s, d)
    mem2 = mem_prompt.reshape(b * mem_len, d)
    P = {k: [] for k in ('fox_k', 'fox_v', 'logf', 'ckv', 'kr', 'dsa_k', 'dsa_v', 'ik', 'mem_k', 'mem_v')}
    S = {k: [] for k in ('fox_k', 'fox_v', 'logf', 'ckv', 'kr', 'dsa_k', 'dsa_v', 'ik')}

    def col(h2, name, width=None):
        off = _LAYOUT[name][0]
        return h2[:, off:off + (width or _LAYOUT[name][1])]

    for l in range(depth):
        params = (bf_pad[l], qn_pad[l], kvn[l], wq_nope[l], wq_rope[l], wkt[l])

        h2 = _matmul(xp, w_pad[l], min(1024, b * t), PROJ_TN)
        pr = _prep(h2, tabs_p, params, tr)
        h3 = h2.reshape(b, t, EP)
        r3 = lambda a: a.reshape(b, t, a.shape[-1])
        c, ct = _cumsum(r3(pr['logf']))
        fox_o = _fox_prompt(h3, c, ct, tq_fox, tk)
        mla_o = _mla_prompt(r3(pr['qlat']), r3(pr['qrope']), r3(pr['ckv']), r3(pr['kr']), wv_slot[l], tq, tk_mla)
        dsa_o = _dsa_prompt(r3(pr['iq']), r3(pr['iw']), r3(pr['ik']), r3(pr['dq']), r3(pr['dk']), h3, tq_dsa, tk)
        mkv = _matmul(mem2, w_mem[l], min(1024, b * mem_len), 2 * MEM_W).reshape(b, mem_len, 2 * MEM_W)
        tqm = min(256, t)
        mem_o = _mem_attention(
            h3, pl.BlockSpec((1, tqm, MEM_W), lambda bi, i: (bi, i, _blk('mem_q'))),
            mkv, pl.BlockSpec((1, mem_len, MEM_W), lambda bi, i: (bi, 0, 0)),
            mkv, pl.BlockSpec((1, mem_len, MEM_W), lambda bi, i: (bi, 0, 1)),
            (b, t // tqm), (b, t, MEM_W), pl.BlockSpec((1, tqm, MEM_W), lambda bi, i: (bi, i, 0)))
        xp_new = _finish(xp, h2, fox_o.reshape(b * t, -1), mla_o.reshape(b * t, -1), dsa_o.reshape(b * t, -1),
                         mem_o.reshape(b * t, -1), wb[l], wo[l], ln_g[l], ln_b[l], tr, alpha)
        P['fox_k'].append(col(h2, 'fox_k')); P['fox_v'].append(col(h2, 'fox_v')); P['logf'].append(pr['logf'])
        P['ckv'].append(pr['ckv']); P['kr'].append(pr['kr'][:, :MLA_ROPE]); P['dsa_k'].append(pr['dk'])
        P['dsa_v'].append(col(h2, 'dsa_v')); P['ik'].append(pr['ik'][:, :IDX_DIM])
        P['mem_k'].append(mkv[..., :MEM_W]); P['mem_v'].append(mkv[..., MEM_W:])
        xp = xp_new

        g2 = _matmul(xs, w_pad[l], n_s, PROJ_TN)
        sr = _prep(g2, tabs_s, params, n_s)
        rows_of = lambda a: a.reshape(bs, rows, a.shape[-1] // N_HEADS)
        new3 = lambda a: a.reshape(bs, ts, a.shape[-1])
        fold = lambda o: jnp.where(((jnp.arange(rows) % N_HEADS) // GROUP == 0)[None, :, None],
                                   o[..., :HEAD_DIM], o[..., HEAD_DIM:]).reshape(n_s, N_HEADS * HEAD_DIM)
        lf_new = new3(sr['logf'])
        lrows = jnp.broadcast_to(jnp.transpose(lf_new, (0, 2, 1))[:, None], (bs, ts, FOX_HEADS, ts)).reshape(bs, rows, ts)
        lrows = jnp.pad(lrows, ((0, 0), (0, 0), (0, LANES - ts)))
        fox_s = fold(_fox_sample(page_table, rows_of(col(g2, 'fox_q')), lrows,
                                 _new_page_t(new3(col(g2, 'fox_k'))), _new_page_t(new3(col(g2, 'fox_v'))),
                                 _new_page_t(lf_new), c_fkt, c_fvt, c_lft, l, pg))
        olat = _mla_sample(page_table, rows_of(sr['qlat']), rows_of(sr['qrope'])[..., :MLA_ROPE],
                           _new_page_t(new3(sr['ckv'])).swapaxes(1, 2), _new_page_t(new3(sr['kr'][:, :MLA_ROPE])),
                           cache_mla_ckv, c_krt, l, pg)
        mla_s = _mla_up(olat, wv_all[l]).reshape(bs, rows, MLA_HEADS, MLA_V).sum(axis=2).reshape(n_s, MLA_W)
        bias = _dsa_idx_sample(page_table, rows_of(sr['iq'])[..., :IDX_DIM], sr['iw'].reshape(bs, rows, 1),
                               _new_page_t(new3(sr['ik'][:, :IDX_DIM])), c_ixt, l, pg, topk_s)
        dsa_s = fold(_dsa_sample(page_table, rows_of(sr['dq']), _new_page_t(new3(sr['dk'])),
                                 _new_page_t(new3(col(g2, 'dsa_v'))), bias, c_dkt, c_dvt, l, pg))
        mem_s = _mem_attention(
            new3(col(g2, 'mem_q')), pl.BlockSpec((1, ts, MEM_W), lambda bi: (bi, 0, 0)),
            c_mk, pl.BlockSpec((None, None, mem_len, MEM_W), lambda bi, l=l: (bi, l, 0, 0)),
            c_mv, pl.BlockSpec((None, None, mem_len, MEM_W), lambda bi, l=l: (bi, l, 0, 0)),
            (bs,), (bs, ts, MEM_W), pl.BlockSpec((1, ts, MEM_W), lambda bi: (bi, 0, 0))).reshape(n_s, MEM_W)
        xs_new = _finish(xs, g2, fox_s, mla_s, dsa_s, mem_s, wb[l], wo[l], ln_g[l], ln_b[l], n_s, alpha)
        S['fox_k'].append(col(g2, 'fox_k')); S['fox_v'].append(col(g2, 'fox_v')); S['logf'].append(sr['logf'])
        S['ckv'].append(sr['ckv']); S['kr'].append(sr['kr'][:, :MLA_ROPE]); S['dsa_k'].append(sr['dk'])
        S['dsa_v'].append(col(g2, 'dsa_v')); S['ik'].append(sr['ik'][:, :IDX_DIM])
        xs = xs_new

    def stack(lst, lead, tail):
        return jnp.stack([a.reshape(lead + tail) for a in lst], axis=1)

    kv2 = (FOX_KV_HEADS, HEAD_DIM)
    pl_, sl_ = (b, t), (bs, ts)
    return (xp.reshape(b, t, d), xs.reshape(bs, ts, d),
            stack(P['fox_k'], pl_, kv2), stack(P['fox_v'], pl_, kv2), stack(P['logf'], pl_, (FOX_HEADS,)),
            stack(P['ckv'], pl_, (MLA_KV_LORA,)), stack(P['kr'], pl_, (MLA_ROPE,)),
            stack(P['dsa_k'], pl_, kv2), stack(P['dsa_v'], pl_, kv2), stack(P['ik'], pl_, (IDX_DIM,)),
            stack(P['mem_k'], (b, mem_len), (MEM_HEADS, HEAD_DIM)), stack(P['mem_v'], (b, mem_len), (MEM_HEADS, HEAD_DIM)),
            stack(S['fox_k'], sl_, kv2), stack(S['fox_v'], sl_, kv2), stack(S['logf'], sl_, (FOX_HEADS,)),
            stack(S['ckv'], sl_, (MLA_KV_LORA,)), stack(S['kr'], sl_, (MLA_ROPE,)),
            stack(S['dsa_k'], sl_, kv2), stack(S['dsa_v'], sl_, kv2), stack(S['ik'], sl_, (IDX_DIM,)))
```
